```python
import jax
import jax.numpy as jnp
from jax import lax
import numpy as np


D_MODEL = 1024
BATCH = 8
SEQ = 4096
DEPTH = 4

GRID_W = 64
CTX_LEN = 256

A_HEAD_DIM = 64
A_HEADS = D_MODEL // (2 * A_HEAD_DIM)
A_KV_HEADS = A_HEADS // 4
WINDOW = 128
A_BLOCK = WINDOW
ROPE_BASE = 10000.0
ROPE_AXIS = A_HEAD_DIM // 2

B_DV = 64
B_DK = B_DV // 2
B_HEADS = D_MODEL // (4 * B_DV)
GLA_RANK = 16
GLA_TAU = 16.0
GLA_CHUNK = 16

C_DH = 64
C_HEADS = D_MODEL // (4 * C_DH)
MLSTM_CHUNK = 64
MLSTM_CONV = 3

A_OUT = A_HEADS * A_HEAD_DIM
A_KV = A_KV_HEADS * A_HEAD_DIM
B_KW = B_HEADS * B_DK
B_VW = B_HEADS * B_DV
C_W = C_HEADS * C_DH
MIX_WIDTH = A_OUT + B_VW + C_W
IN_SPLITS = (A_OUT, A_KV, A_KV, B_KW, B_KW, B_VW, B_VW, GLA_RANK, GLA_RANK, C_W, C_W, C_W, 4 * C_HEADS, C_W)
IN_WIDTH = A_OUT + 2 * A_KV + 2 * B_KW + 2 * B_VW + 2 * GLA_RANK + 4 * C_W + 4 * C_HEADS

D_FF = 2816
FFN_RES = 0.5
N_MOD = 9
EPS = 1e-6

kernel_name = 'hybrid_headgroup_diffusion_block'


def rms_norm(x, g):
    xf = x.astype(jnp.float32)
    y = xf * lax.rsqrt(jnp.mean(xf * xf, axis=-1, keepdims=True) + EPS)
    return (y * g.astype(jnp.float32)).astype(x.dtype)


def modulate(x, g, shift, scale):
    return rms_norm(x, g) * (1.0 + scale) + shift


def adaln_params(cvec, w, b):
    m = jax.nn.silu(cvec) @ w + b
    return m.reshape(m.shape[:-1] + (N_MOD, D_MODEL))


def swiglu(u, wg, wu, wd):
    return (jax.nn.silu(u @ wg) * (u @ wu)) @ wd


def ffn_sublayer(x, shift, scale, gate, g_pre, g_post, wg, wu, wd):
    y = swiglu(modulate(x, g_pre, shift, scale), wg, wu, wd)
    return x + FFN_RES * gate * rms_norm(y, g_post)


def split_cols(z):
    parts = []
    start = 0
    for w in IN_SPLITS:
        parts.append(z[..., start:start + w])
        start += w
    return parts


def to_heads(t, n_heads):
    b, t_len, _ = t.shape
    return t.reshape(b, t_len, n_heads, -1).transpose(0, 2, 1, 3)


def from_heads(t):
    b, h, t_len, d = t.shape
    return t.transpose(0, 2, 1, 3).reshape(b, t_len, h * d)


def head_layernorm(o, g):
    of = o.astype(jnp.float32)
    mu = jnp.mean(of, axis=-1, keepdims=True)
    var = jnp.mean(jnp.square(of - mu), axis=-1, keepdims=True)
    y = from_heads((of - mu) * lax.rsqrt(var + EPS))
    return (y * g.astype(jnp.float32)).astype(o.dtype)


def centred_dwconv(x, w):
    k_w = w.shape[0]
    pad = k_w // 2
    t_len = x.shape[1]
    xp = jnp.pad(x, ((0, 0), (pad, pad), (0, 0)))
    out = xp[:, 0:t_len] * w[0]
    for j in range(1, k_w):
        out = out + xp[:, j:j + t_len] * w[j]
    return out


def axial_rope_tables(row_ids, col_ids):
    half = ROPE_AXIS // 2
    inv_freq = ROPE_BASE ** (-jnp.arange(half, dtype=jnp.float32) / half)
    ang_r = row_ids.astype(jnp.float32)[:, None] * inv_freq[None, :]
    ang_c = col_ids.astype(jnp.float32)[:, None] * inv_freq[None, :]
    return (jnp.cos(ang_r)[:, None, :], jnp.sin(ang_r)[:, None, :],
            jnp.cos(ang_c)[:, None, :], jnp.sin(ang_c)[:, None, :])


def axial_rope(x, cos_r, sin_r, cos_c, sin_c):
    def rot(u, cs, sn):
        cs = cs.astype(u.dtype)
        sn = sn.astype(u.dtype)
        hh = u.shape[-1] // 2
        u1, u2 = u[..., :hh], u[..., hh:]
        return jnp.concatenate([u1 * cs - u2 * sn, u1 * sn + u2 * cs], axis=-1)
    return jnp.concatenate([rot(x[..., :ROPE_AXIS], cos_r, sin_r),
                            rot(x[..., ROPE_AXIS:], cos_c, sin_c)], axis=-1)


def windowed_attention(q, k, v, kc, vc, sink):
    b, s_len, hq, d = q.shape
    g = k.shape[2]
    r = hq // g
    nb = s_len // A_BLOCK
    scale = d ** -0.5
    qb = q.reshape(b, nb, A_BLOCK, g, r, d)
    pad = ((0, 0), (A_BLOCK, A_BLOCK), (0, 0), (0, 0))
    kp = jnp.pad(k, pad).reshape(b, nb + 2, A_BLOCK, g, d)
    vp = jnp.pad(v, pad).reshape(b, nb + 2, A_BLOCK, g, d)
    kw = jnp.concatenate([kp[:, :-2], kp[:, 1:-1], kp[:, 2:]], axis=2)
    vw = jnp.concatenate([vp[:, :-2], vp[:, 1:-1], vp[:, 2:]], axis=2)
    qi = jnp.arange(A_BLOCK)[:, None]
    kj = jnp.arange(3 * A_BLOCK)[None, :]
    band = jnp.abs(qi + A_BLOCK - kj) <= WINDOW
    k_abs = (jnp.arange(nb)[:, None] - 1) * A_BLOCK + kj
    mask = band[None] & ((k_abs >= 0) & (k_abs < s_len))[:, None, :]
    s_win = jnp.einsum('bnqgrd,bnkgd->bgrnqk', qb, kw).astype(jnp.float32) * scale
    s_win = jnp.where(mask, s_win, -jnp.inf)
    s_ctx = jnp.einsum('bnqgrd,bcgd->bgrnqc', qb, kc).astype(jnp.float32) * scale
    s_sink = sink.astype(jnp.float32).reshape(1, g, r, 1, 1)
    m = jnp.maximum(jnp.maximum(jnp.max(s_win, axis=-1), jnp.max(s_ctx, axis=-1)), s_sink)
    e_win = jnp.exp(s_win - m[..., None])
    e_ctx = jnp.exp(s_ctx - m[..., None])
    den = jnp.sum(e_win, axis=-1) + jnp.sum(e_ctx, axis=-1) + jnp.exp(s_sink - m)
    o = (jnp.einsum('bgrnqk,bnkgd->bgrnqd', e_win.astype(v.dtype), vw)
         + jnp.einsum('bgrnqc,bcgd->bgrnqd', e_ctx.astype(v.dtype), vc))
    o = o / den[..., None]
    return o.transpose(0, 3, 4, 1, 2, 5).reshape(b, s_len, hq * d).astype(v.dtype)


def context_attention(qc, kc, vc, sink):
    b, l_c, hq, d = qc.shape
    g = kc.shape[2]
    r = hq // g
    s = jnp.einsum('blgrd,bcgd->bgrlc', qc.reshape(b, l_c, g, r, d), kc).astype(jnp.float32) * d ** -0.5
    s_sink = jnp.broadcast_to(sink.astype(jnp.float32).reshape(1, g, r, 1, 1), s.shape[:-1] + (1,))
    p = jax.nn.softmax(jnp.concatenate([s, s_sink], axis=-1), axis=-1)[..., :l_c]
    o = jnp.einsum('bgrlc,bcgd->blgrd', p.astype(vc.dtype), vc)
    return o.reshape(b, l_c, hq * d)


def gla_chunked(q, k, v, log_a, state):
    b, h, t_len, dk = q.shape
    dv = v.shape[-1]
    lc = GLA_CHUNK
    n = t_len // lc
    qf = q.astype(jnp.float32).reshape(b, h, n, lc, dk)
    kf = k.astype(jnp.float32).reshape(b, h, n, lc, dk)
    vf = v.astype(jnp.float32).reshape(b, h, n, lc, dv)
    bcum = jnp.cumsum(log_a.astype(jnp.float32).reshape(b, h, n, lc, dk), axis=3)
    b_last = bcum[:, :, :, -1]
    causal = jnp.tril(jnp.ones((lc, lc), dtype=bool))[:, :, None]
    decay = jnp.exp(jnp.where(causal, bcum[:, :, :, :, None, :] - bcum[:, :, :, None, :, :], -jnp.inf))
    attn = jnp.einsum('bhntd,bhnsd,bhntsd->bhnts', qf, kf, decay)
    o_intra = jnp.einsum('bhnts,bhnsv->bhntv', attn, vf)
    s_loc = jnp.einsum('bhnsd,bhnsv->bhndv', kf * jnp.exp(b_last[:, :, :, None, :] - bcum), vf)

    def step(s, inp):
        bl, sl = inp
        return jnp.exp(bl)[..., None] * s + sl, s

    s_final, s_prev = lax.scan(step, state, (jnp.moveaxis(b_last, 2, 0), jnp.moveaxis(s_loc, 2, 0)))
    s_prev = jnp.moveaxis(s_prev, 0, 2)
    o = o_intra + jnp.einsum('bhntd,bhndv->bhntv', qf * jnp.exp(bcum), s_prev)
    return o.reshape(b, h, t_len, dv).astype(v.dtype), s_final


def mlstm_chunked(q, k, v, log_i, log_f, state):
    b, h, t_len, d = q.shape
    lc = MLSTM_CHUNK
    n_ch = t_len // lc
    qf = q.astype(jnp.float32).reshape(b, h, n_ch, lc, d)
    kf = k.astype(jnp.float32).reshape(b, h, n_ch, lc, d)
    vf = v.astype(jnp.float32).reshape(b, h, n_ch, lc, d)
    li = log_i.astype(jnp.float32).reshape(b, h, n_ch, lc)
    bcum = jnp.cumsum(log_f.astype(jnp.float32).reshape(b, h, n_ch, lc), axis=-1)
    b_last = bcum[..., -1]
    causal = jnp.tril(jnp.ones((lc, lc), dtype=bool))
    dmat = jnp.where(causal, bcum[..., :, None] - bcum[..., None, :] + li[..., None, :], -jnp.inf)
    g_end = b_last[..., None] - bcum + li
    m_loc = jnp.max(g_end, axis=-1)
    w_end = jnp.exp(g_end - m_loc[..., None])
    c_loc = jnp.einsum('bhns,bhnsv,bhnsk->bhnvk', w_end, vf, kf)
    n_loc = jnp.einsum('bhns,bhnsk->bhnk', w_end, kf)

    def step(carry, inp):
        c_st, n_st, m_st = carry
        bl, ml, cl, nl = inp
        m_new = jnp.maximum(bl + m_st, ml)
        a = jnp.exp(bl + m_st - m_new)
        e = jnp.exp(ml - m_new)
        c_new = a[..., None, None] * c_st + e[..., None, None] * cl
        n_new = a[..., None] * n_st + e[..., None] * nl
        return (c_new, n_new, m_new), (c_st, n_st, m_st)

    xs = (jnp.moveaxis(b_last, 2, 0), jnp.moveaxis(m_loc, 2, 0), jnp.moveaxis(c_loc, 2, 0), jnp.moveaxis(n_loc, 2, 0))
    final, (c_prev, n_prev, m_prev) = lax.scan(step, state, xs)
    c_prev = jnp.moveaxis(c_prev, 0, 2)
    n_prev = jnp.moveaxis(n_prev, 0, 2)
    m_prev = jnp.moveaxis(m_prev, 0, 2)
    inter = bcum + m_prev[..., None]
    m_t = jnp.maximum(inter, jnp.max(dmat, axis=-1))
    a_t = jnp.exp(inter - m_t)
    sc = jnp.einsum('bhntd,bhnsd->bhnts', qf, kf) * jnp.exp(dmat - m_t[..., None])
    num = a_t[..., None] * jnp.einsum('bhntk,bhnvk->bhntv', qf, c_prev) + jnp.einsum('bhnts,bhnsv->bhntv', sc, vf)
    den = a_t * jnp.einsum('bhntk,bhnk->bhnt', qf, n_prev) + jnp.sum(sc, axis=-1)
    hid = num / jnp.maximum(jnp.abs(den), jnp.exp(-m_t))[..., None]
    return hid.reshape(b, h, t_len, d).astype(v.dtype), final


def bidirectional(scan_fn, state0, lat, ctx, need_ctx):
    def flip(ts):
        return tuple(jnp.flip(t, axis=2) for t in ts)
    (cqkv, cgf, cgb), (lqkv, lgf, lgb) = ctx, lat
    oc_f, sc_f = scan_fn(*cqkv, *cgf, state0)
    oc_b, sc_b = scan_fn(*flip(cqkv), *flip(cgb), state0)
    o_f, _ = scan_fn(*lqkv, *lgf, sc_f)
    o_b, _ = scan_fn(*flip(lqkv), *flip(lgb), sc_b)
    o = o_f + jnp.flip(o_b, axis=2)
    oc = oc_f + jnp.flip(oc_b, axis=2) if need_ctx else None
    return o, oc


def token_mixing(u, uc, rope, w_in, b_in, w_out, sink, gla_wa, gla_ba, gla_g, conv_w, f_bias, mlstm_g, need_ctx):
    b, t_len, _ = u.shape
    l_c = uc.shape[1]
    pl = split_cols(u @ w_in + b_in)
    pc = split_cols(uc @ w_in + b_in)

    q = axial_rope(pl[0].reshape(b, t_len, A_HEADS, A_HEAD_DIM), *rope)
    k = axial_rope(pl[1].reshape(b, t_len, A_KV_HEADS, A_HEAD_DIM), *rope)
    v = pl[2].reshape(b, t_len, A_KV_HEADS, A_HEAD_DIM)
    kc = pc[1].reshape(b, l_c, A_KV_HEADS, A_HEAD_DIM)
    vc = pc[2].reshape(b, l_c, A_KV_HEADS, A_HEAD_DIM)
    y_a = windowed_attention(q, k, v, kc, vc, sink)

    def gla_inputs(p):
        qkv = (to_heads(p[3], B_HEADS) * B_DK ** -0.5, to_heads(p[4], B_HEADS), to_heads(p[5], B_HEADS))
        def gate(dirn):
            z = (p[7 + dirn] @ gla_wa[dirn] + gla_ba[dirn]).astype(jnp.float32)
            return (to_heads(jax.nn.log_sigmoid(z) / GLA_TAU, B_HEADS),)
        return qkv, gate(0), gate(1)
    gla_state0 = jnp.zeros((b, B_HEADS, B_DK, B_DV), jnp.float32)
    o_b, oc_b = bidirectional(gla_chunked, gla_state0, gla_inputs(pl), gla_inputs(pc), need_ctx)
    y_b = jax.nn.silu(pl[6]) * head_layernorm(o_b, gla_g)

    def mlstm_inputs(p):
        qk = jax.nn.silu(centred_dwconv(jnp.concatenate([p[9], p[10]], axis=-1), conv_w))
        qkv = (to_heads(qk[..., :C_W], C_HEADS), to_heads(qk[..., C_W:], C_HEADS) * C_DH ** -0.5, to_heads(p[11], C_HEADS))
        gt = p[12].astype(jnp.float32).reshape(p[12].shape[:2] + (4, C_HEADS)).transpose(2, 0, 3, 1)
        fwd = (gt[0], jax.nn.log_sigmoid(gt[1] + f_bias[0][None, :, None]))
        bwd = (gt[2], jax.nn.log_sigmoid(gt[3] + f_bias[1][None, :, None]))
        return qkv, fwd, bwd
    mstate0 = (jnp.zeros((b, C_HEADS, C_DH, C_DH), jnp.float32),
               jnp.zeros((b, C_HEADS, C_DH), jnp.float32),
               jnp.zeros((b, C_HEADS), jnp.float32))
    o_c, oc_c = bidirectional(mlstm_chunked, mstate0, mlstm_inputs(pl), mlstm_inputs(pc), need_ctx)
    y_c = jax.nn.sigmoid(pl[13]) * head_layernorm(o_c, mlstm_g)

    y = jnp.concatenate([y_a, y_b, y_c], axis=-1) @ w_out
    if not need_ctx:
        return y, None
    qc = pc[0].reshape(b, l_c, A_HEADS, A_HEAD_DIM)
    yc_a = context_attention(qc, kc, vc, sink)
    yc_b = jax.nn.silu(pc[6]) * head_layernorm(oc_b, gla_g)
    yc_c = jax.nn.sigmoid(pc[13]) * head_layernorm(oc_c, mlstm_g)
    yc = jnp.concatenate([yc_a, yc_b, yc_c], axis=-1) @ w_out
    return y, yc


def setup_inputs(seed: int = 0) -> dict:
    key = jax.random.key(seed)
    ks = jax.random.split(key, 24)

    def nrm(k, shape, s):
        return jax.random.normal(k, shape, jnp.float32) * s

    n_l = DEPTH
    return {
        'x': nrm(ks[0], (BATCH, SEQ, D_MODEL), 1.0),
        'c': nrm(ks[1], (BATCH, D_MODEL), 1.0),
        'ctx': nrm(ks[2], (BATCH, CTX_LEN, D_MODEL), 1.0),
        'c_ctx': nrm(ks[3], (D_MODEL,), 1.0),
        'ada_w': nrm(ks[4], (n_l, D_MODEL, N_MOD * D_MODEL), D_MODEL ** -0.5),
        'ada_b': nrm(ks[5], (n_l, N_MOD * D_MODEL), 0.02),
        'norm_g': 1.0 + nrm(ks[6], (n_l, 6, D_MODEL), 0.02),
        'ffn1_wg': nrm(ks[7], (n_l, D_MODEL, D_FF), D_MODEL ** -0.5),
        'ffn1_wu': nrm(ks[8], (n_l, D_MODEL, D_FF), D_MODEL ** -0.5),
        'ffn1_wd': nrm(ks[9], (n_l, D_FF, D_MODEL), D_FF ** -0.5),
        'ffn2_wg': nrm(ks[10], (n_l, D_MODEL, D_FF), D_MODEL ** -0.5),
        'ffn2_wu': nrm(ks[11], (n_l, D_MODEL, D_FF), D_MODEL ** -0.5),
        'ffn2_wd': nrm(ks[12], (n_l, D_FF, D_MODEL), D_FF ** -0.5),
        'w_in': nrm(ks[13], (n_l, D_MODEL, IN_WIDTH), D_MODEL ** -0.5),
        'b_in': nrm(ks[14], (n_l, IN_WIDTH), 0.02),
        'w_out': nrm(ks[15], (n_l, MIX_WIDTH, D_MODEL), MIX_WIDTH ** -0.5),
        'attn_sink': nrm(ks[16], (n_l, A_HEADS), 0.5),
        'gla_w_alpha': nrm(ks[17], (n_l, 2, GLA_RANK, B_KW), GLA_RANK ** -0.5),
        'gla_b_alpha': nrm(ks[18], (n_l, 2, B_KW), 0.1),
        'gla_norm': 1.0 + nrm(ks[19], (n_l, B_VW), 0.02),
        'mlstm_conv': nrm(ks[20], (n_l, MLSTM_CONV, 2 * C_W), MLSTM_CONV ** -0.5),
        'mlstm_f_bias': jnp.linspace(3.0, 6.0, C_HEADS, dtype=jnp.float32) + nrm(ks[21], (n_l, 2, C_HEADS), 0.1),
        'mlstm_norm': 1.0 + nrm(ks[22], (n_l, C_W), 0.02),
    }


def reference(x, c, ctx, c_ctx, ada_w, ada_b, norm_g, ffn1_wg, ffn1_wu, ffn1_wd, ffn2_wg, ffn2_wu, ffn2_wd,
              w_in, b_in, w_out, attn_sink, gla_w_alpha, gla_b_alpha, gla_norm, mlstm_conv, mlstm_f_bias, mlstm_norm):
    s_len = x.shape[1]
    rows = s_len // GRID_W
    row_ids = jnp.broadcast_to(jnp.arange(rows)[:, None], (rows, GRID_W)).reshape(-1)
    col_ids = jnp.broadcast_to(jnp.arange(GRID_W)[None, :], (rows, GRID_W)).reshape(-1)
    rope = axial_rope_tables(row_ids, col_ids)

    h, hc = x, ctx
    for layer in range(DEPTH):
        need_ctx = layer < DEPTH - 1
        mod = adaln_params(c, ada_w[layer], ada_b[layer])[:, :, None, :]
        modc = adaln_params(c_ctx, ada_w[layer], ada_b[layer])
        g = norm_g[layer]

        h = ffn_sublayer(h, mod[:, 0], mod[:, 1], mod[:, 2], g[0], g[1], ffn1_wg[layer], ffn1_wu[layer], ffn1_wd[layer])
        hc = ffn_sublayer(hc, modc[0], modc[1], modc[2], g[0], g[1], ffn1_wg[layer], ffn1_wu[layer], ffn1_wd[layer])

        y, yc = token_mixing(modulate(h, g[2], mod[:, 3], mod[:, 4]), modulate(hc, g[2], modc[3], modc[4]), rope,
                             w_in[layer], b_in[layer], w_out[layer], attn_sink[layer],
                             gla_w_alpha[layer], gla_b_alpha[layer], gla_norm[layer],
                             mlstm_conv[layer], mlstm_f_bias[layer], mlstm_norm[layer], need_ctx)
        h = h + mod[:, 5] * rms_norm(y, g[3])

        h = ffn_sublayer(h, mod[:, 6], mod[:, 7], mod[:, 8], g[4], g[5], ffn2_wg[layer], ffn2_wu[layer], ffn2_wd[layer])
        if need_ctx:
            hc = hc + modc[5] * rms_norm(yc, g[3])
            hc = ffn_sublayer(hc, modc[6], modc[7], modc[8], g[4], g[5], ffn2_wg[layer], ffn2_wu[layer], ffn2_wd[layer])
    return h
```

```python
import functools

import jax
import jax.numpy as jnp
from jax import lax
from jax.experimental import pallas as pl
from jax.experimental.pallas import tpu as pltpu

F32 = jnp.float32
BF16 = jnp.bfloat16

GRID_W = 64
A_HEAD_DIM = 64
A_HEADS = 8
A_KV_HEADS = 2
A_REP = A_HEADS // A_KV_HEADS
WINDOW = 128
ROPE_BASE = 10000.0
ROPE_AXIS = A_HEAD_DIM // 2
B_DK = 32
B_DV = 64
B_HEADS = 4
GLA_RANK = 16
GLA_TAU = 16.0
GLA_CHUNK = 16
C_DH = 64
C_HEADS = 4
A_OUT = A_HEADS * A_HEAD_DIM
A_KV = A_KV_HEADS * A_HEAD_DIM
B_KW = B_HEADS * B_DK
B_VW = B_HEADS * B_DV
C_W = C_HEADS * C_DH
FFN_RES = 0.5
N_MOD = 9
EPS = 1e-6

PA_W = A_OUT + 2 * A_KV
PB_W = 2 * B_KW + 2 * B_VW
PC_W = 4 * C_W
PS_W = 128
PS_GATE0 = 2 * GLA_RANK

LANES = 128
SUBLANES = 8
SEQ_BLOCK = 256
ATT_BLOCK = WINDOW
FF_CHUNK = 256
V7X_VMEM_BYTES = 64 * 1024 * 1024
VMEM_LIMIT = V7X_VMEM_BYTES * 7 // 8


def _sigmoid(x):
    return 1.0 / (1.0 + jnp.exp(-x))


def _silu(x):
    return x * _sigmoid(x)


def _log_sigmoid(x):
    return jnp.minimum(x, 0.0) - jnp.log(1.0 + jnp.exp(-jnp.abs(x)))


def _rms(x, g):
    return x * lax.rsqrt(jnp.mean(x * x, axis=-1, keepdims=True) + EPS) * g


def _dot(a, b):
    return jnp.dot(a.astype(BF16), b.astype(BF16), preferred_element_type=F32)


def _dot_nt(a, b):
    return lax.dot_general(a.astype(BF16), b.astype(BF16), (((1,), (1,)), ((), ())),
                           preferred_element_type=F32)


def _split(x, parts):
    out = []
    for _ in range(parts):
        t = x.astype(BF16)
        out.append(t)
        x = x - t.astype(F32)
    return out


def _dot_sel(sel, x, parts=3):
    sel = sel.astype(BF16)
    acc = None
    for t in _split(x, parts):
        y = jnp.dot(sel, t, preferred_element_type=F32)
        acc = y if acc is None else acc + y
    return acc


def _dot_sel_rhs(x, sel, parts=3):
    sel = sel.astype(BF16)
    acc = None
    for t in _split(x, parts):
        y = jnp.dot(t, sel, preferred_element_type=F32)
        acc = y if acc is None else acc + y
    return acc


def _dot_sel_nt(x, sel, parts=3):
    sel = sel.astype(BF16)
    acc = None
    for t in _split(x, parts):
        y = lax.dot_general(t, sel, (((1,), (1,)), ((), ())), preferred_element_type=F32)
        acc = y if acc is None else acc + y
    return acc


def _sel_dot_nt(sel, x, parts=3):
    sel = sel.astype(BF16)
    acc = None
    for t in _split(x, parts):
        y = lax.dot_general(sel, t, (((1,), (1,)), ((), ())), preferred_element_type=F32)
        acc = y if acc is None else acc + y
    return acc


def _iota(shape, dim):
    return lax.broadcasted_iota(jnp.int32, shape, dim)


def _div(x, n):
    assert n & (n - 1) == 0
    return x >> (n.bit_length() - 1)


def _params(*semantics):
    return pltpu.CompilerParams(dimension_semantics=semantics, vmem_limit_bytes=VMEM_LIMIT)


def _ada_kernel(c_ref, w_ref, b_ref, o_ref):
    o_ref[...] = _dot(_silu(c_ref[...]), w_ref[...]) + b_ref[...]


def _ada_table(cc, ada_w, ada_b):
    n_l, d, _ = ada_w.shape
    r = cc.shape[0]
    out = pl.pallas_call(
        _ada_kernel,
        out_shape=jax.ShapeDtypeStruct((n_l, N_MOD, r, d), F32),
        grid=(n_l, N_MOD),
        in_specs=[
            pl.BlockSpec((r, d), lambda l, j: (0, 0)),
            pl.BlockSpec((None, d, d), lambda l, j: (l, 0, j)),
            pl.BlockSpec((None, 1, d), lambda l, j: (l, 0, j)),
        ],
        out_specs=pl.BlockSpec((None, None, r, d), lambda l, j: (l, j, 0, 0)),
        compiler_params=_params("parallel", "parallel"),
        name="ada",
    )(cc, ada_w, ada_b.reshape(n_l, 1, N_MOD * d))
    return out.transpose(0, 2, 1, 3)


class _Geom:
    def __init__(self, batch, seq, ctx_len, tm):
        self.batch, self.seq, self.ctx_len, self.tm = batch, seq, ctx_len, tm
        self.n_ctx = batch * ctx_len
        self.n_tok = batch * (ctx_len + seq)
        assert self.n_ctx % tm == 0 and seq % tm == 0
        self.ctx_tiles = self.n_ctx // tm
        self.tiles_per_batch = seq // tm
        self.tiles = self.n_tok // tm

    def mod_row(self, i):
        return jnp.where(i < self.ctx_tiles, self.batch, (i - self.ctx_tiles) // self.tiles_per_batch)

    def rope_block(self, i):
        return jnp.where(i < self.ctx_tiles, self.tiles_per_batch,
                         (i - self.ctx_tiles) % self.tiles_per_batch)


def _ffn_kernel(x_ref, mod_ref, g_ref, wg_ref, wu_ref, wd_ref, o_ref, *, mod_base, g_base):
    x = x_ref[...]
    mod = mod_ref[...]
    g = g_ref[...]
    shift = mod[mod_base:mod_base + 1]
    scale = mod[mod_base + 1:mod_base + 2]
    gate = mod[mod_base + 2:mod_base + 3]
    u = (_rms(x, g[g_base:g_base + 1]) * (1.0 + scale) + shift).astype(BF16)
    d_ff = wg_ref.shape[1]
    y = None
    for c0 in range(0, d_ff, FF_CHUNK):
        a = jnp.dot(u, wg_ref[:, c0:c0 + FF_CHUNK], preferred_element_type=F32)
        b = jnp.dot(u, wu_ref[:, c0:c0 + FF_CHUNK], preferred_element_type=F32)
        hcol = (_silu(a) * b).astype(BF16)
        part = jnp.dot(hcol, wd_ref[c0:c0 + FF_CHUNK, :], preferred_element_type=F32)
        y = part if y is None else y + part
    o_ref[...] = x + FFN_RES * gate * _rms(y, g[g_base + 1:g_base + 2])


def _ffn(tok, modtab, norm_g, wg, wu, wd, geom, layer, mod_base, g_base, latent_only=False):
    tm = geom.tm
    d = tok.shape[1]
    d_ff = wg.shape[2]
    assert d_ff % FF_CHUNK == 0
    first = geom.ctx_tiles if latent_only else 0
    n_tiles = geom.tiles - first
    return pl.pallas_call(
        functools.partial(_ffn_kernel, mod_base=mod_base, g_base=g_base),
        out_shape=jax.ShapeDtypeStruct((n_tiles * tm, d), F32),
        grid=(n_tiles,),
        in_specs=[
            pl.BlockSpec((tm, d), lambda i: (i + first, 0)),
            pl.BlockSpec((None, None, N_MOD, d), lambda i: (layer, geom.mod_row(i + first), 0, 0)),
            pl.BlockSpec((None, 6, d), lambda i: (layer, 0, 0)),
            pl.BlockSpec((None, d, d_ff), lambda i: (layer, 0, 0)),
            pl.BlockSpec((None, d, d_ff), lambda i: (layer, 0, 0)),
            pl.BlockSpec((None, d_ff, d), lambda i: (layer, 0, 0)),
        ],
        out_specs=pl.BlockSpec((tm, d), lambda i: (i, 0)),
        compiler_params=_params("parallel"),
        name="ffn",
    )(tok, modtab, norm_g, wg, wu, wd)


def _inproj_kernel(x_ref, mod_ref, g_ref, w_ref, b_ref, cos_ref, sin_ref,
                   pa_ref, pb_ref, pc_ref, ps_ref):
    x = x_ref[...]
    mod = mod_ref[...]
    g = g_ref[...]
    u = _rms(x, g[2:3]) * (1.0 + mod[4:5]) + mod[3:4]
    p = _dot(u, w_ref[...]) + b_ref[...]
    qk_w = A_OUT + A_KV
    qk = p[:, 0:qk_w]
    reps = qk_w // LANES
    cos = jnp.concatenate([cos_ref[...]] * reps, axis=1)
    sin = jnp.concatenate([sin_ref[...]] * reps, axis=1)
    half = ROPE_AXIS // 2
    first_half = (_iota(qk.shape, 1) & (ROPE_AXIS - 1)) < half
    partner = jnp.where(first_half, pltpu.roll(qk, qk_w - half, 1), pltpu.roll(qk, half, 1))
    rot = qk * cos + partner * sin
    pa_ref[:, 0:A_OUT] = rot[:, 0:A_OUT] * (A_HEAD_DIM ** -0.5)
    pa_ref[:, A_OUT:qk_w] = rot[:, A_OUT:qk_w]
    pa_ref[:, qk_w:PA_W] = p[:, qk_w:PA_W]
    pb_ref[...] = p[:, PA_W:PA_W + PB_W]
    pc_ref[...] = p[:, PA_W + PB_W:PA_W + PB_W + PC_W]
    ps_ref[...] = p[:, PA_W + PB_W + PC_W:]


def _inproj(tok, modtab, norm_g, w_in, b_in, rope_cos, rope_sin, geom, layer):
    tm = geom.tm
    d = tok.shape[1]
    n_tok = tok.shape[0]
    width = w_in.shape[2]
    outs = [jax.ShapeDtypeStruct((n_tok, w), F32) for w in (PA_W, PB_W, PC_W, PS_W)]
    return pl.pallas_call(
        _inproj_kernel,
        out_shape=outs,
        grid=(geom.tiles,),
        in_specs=[
            pl.BlockSpec((tm, d), lambda i: (i, 0)),
            pl.BlockSpec((None, None, N_MOD, d), lambda i: (layer, geom.mod_row(i), 0, 0)),
            pl.BlockSpec((None, 6, d), lambda i: (layer, 0, 0)),
            pl.BlockSpec((None, d, width), lambda i: (layer, 0, 0)),
            pl.BlockSpec((None, 1, width), lambda i: (layer, 0, 0)),
            pl.BlockSpec((tm, LANES), lambda i: (geom.rope_block(i), 0)),
            pl.BlockSpec((tm, LANES), lambda i: (geom.rope_block(i), 0)),
        ],
        out_specs=[pl.BlockSpec((tm, w), lambda i: (i, 0)) for w in (PA_W, PB_W, PC_W, PS_W)],
        compiler_params=_params("parallel"),
        name="inproj",
    )(tok, modtab, norm_g, w_in, b_in, rope_cos, rope_sin)


def _place_half(x, src_half, dst_half):
    half = LANES // 2
    if src_half != dst_half:
        x = pltpu.roll(x, half, 1)
    lane = _iota(x.shape, 1)
    keep = (lane >= half) if dst_half else (lane < half)
    return jnp.where(keep, x, 0.0)


def _attn_kernel(q_ref, k0_ref, k1_ref, k2_ref, v0_ref, v1_ref, v2_ref, kc_ref, vc_ref, sink_ref,
                 o_ref, *, ctx_blocks, seq_blocks):
    j = pl.program_id(1)
    n = j - ctx_blocks
    blk = ATT_BLOCK
    n_win = 3 * blk
    qi = _iota((blk, n_win), 0)
    kj = _iota((blk, n_win), 1)
    k_abs = (n - 1) * blk + kj
    hi = jnp.where(n >= 0, seq_blocks * blk, 0)
    mask = (jnp.abs(qi + blk - kj) <= WINDOW) & (k_abs >= 0) & (k_abs < hi)
    k_all = jnp.concatenate([k0_ref[...], k1_ref[...], k2_ref[...], kc_ref[...]], axis=0)
    v_all = jnp.concatenate([v0_ref[...], v1_ref[...], v2_ref[...], vc_ref[...]], axis=0)
    zeros = jnp.zeros_like(v_all)
    for g in range(A_KV_HEADS):
        acc = None
        for r in range(A_REP):
            h = g * A_REP + r
            pair, pos = h // 2, h % 2
            q2 = q_ref[:, pair * LANES:(pair + 1) * LANES]
            s = _dot_nt(q2, _place_half(k_all, g, pos))
            s_win = jnp.where(mask, s[:, :n_win], -jnp.inf)
            s_ctx = s[:, n_win:]
            sk = sink_ref[h:h + 1, 0:1]
            m = jnp.maximum(jnp.maximum(jnp.max(s_win, axis=1, keepdims=True),
                                        jnp.max(s_ctx, axis=1, keepdims=True)), sk)
            e_win = jnp.exp(s_win - m)
            e_ctx = jnp.exp(s_ctx - m)
            den = (jnp.sum(e_win, axis=1, keepdims=True) + jnp.sum(e_ctx, axis=1, keepdims=True)
                   + jnp.exp(sk - m))
            e = jnp.concatenate([e_win, e_ctx], axis=1) / den
            vq = _place_half(v_all, g, r % 2)
            vq = jnp.concatenate([vq, zeros] if r < 2 else [zeros, vq], axis=1)
            part = _dot(e, vq)
            acc = part if acc is None else acc + part
        o_ref[:, g * 2 * LANES:(g + 1) * 2 * LANES] = acc


def _attn(pa, sink_rows, batch, seq, ctx_len):
    n_tok = pa.shape[0]
    blk = ATT_BLOCK
    ctx_blocks = ctx_len // blk
    seq_blocks = seq // blk
    lat0 = batch * ctx_blocks
    k_col = A_OUT // LANES
    v_col = k_col + 1

    def q_map(b, j):
        return (jnp.where(j < ctx_blocks, b * ctx_blocks + j, lat0 + b * seq_blocks + j - ctx_blocks), 0)

    def kv_map(off, col):
        def f(b, j):
            n = jnp.clip(j - ctx_blocks + off, 0, seq_blocks - 1)
            return (lat0 + b * seq_blocks + n, col)
        return f

    kv_specs = [pl.BlockSpec((blk, LANES), kv_map(off, col))
                for col in (k_col, v_col) for off in (-1, 0, 1)]
    return pl.pallas_call(
        functools.partial(_attn_kernel, ctx_blocks=ctx_blocks, seq_blocks=seq_blocks),
        out_shape=jax.ShapeDtypeStruct((n_tok, A_OUT), F32),
        grid=(batch, ctx_blocks + seq_blocks),
        in_specs=[pl.BlockSpec((blk, A_OUT), q_map)] + kv_specs + [
            pl.BlockSpec((ctx_len, LANES), lambda b, j: (b, k_col)),
            pl.BlockSpec((ctx_len, LANES), lambda b, j: (b, v_col)),
            pl.BlockSpec((SUBLANES, LANES), lambda b, j: (0, 0)),
        ],
        out_specs=pl.BlockSpec((blk, A_OUT), q_map),
        compiler_params=_params("parallel", "parallel"),
        name="attn",
    )(pa, pa, pa, pa, pa, pa, pa, pa, pa, sink_rows)


def _seq_block(b, j, reverse, batch, ctx_blocks, seq_blocks):
    jc = jnp.where(reverse, ctx_blocks - 1 - j, j)
    jl = jnp.where(reverse, seq_blocks - 1 - (j - ctx_blocks), j - ctx_blocks)
    return jnp.where(j < ctx_blocks, b * ctx_blocks + jc, batch * ctx_blocks + b * seq_blocks + jl)


def _gla_kernel(pb_ref, ps_ref, wa_ref, ba_ref, o_ref, st_ref, *, reverse):
    @pl.when(pl.program_id(1) == 0)
    def _():
        st_ref[...] = jnp.zeros_like(st_ref)

    blk = SEQ_BLOCK
    ch = GLA_CHUNK
    n_ch = blk // ch
    shift = ch.bit_length() - 1
    q = pb_ref[:, 0:B_KW] * (B_DK ** -0.5)
    k = pb_ref[:, B_KW:2 * B_KW]
    v = pb_ref[:, 2 * B_KW:2 * B_KW + B_VW]
    log_a = _log_sigmoid(_dot(ps_ref[...], wa_ref[...]) + ba_ref[...]) * (1.0 / GLA_TAU)

    r = _iota((blk, blk), 0)
    c = _iota((blk, blk), 1)
    same = (r >> shift) == (c >> shift)
    tri = same & ((c >= r) if reverse else (c <= r))
    bcum = _dot_sel(tri.astype(F32), log_a)
    btot = _dot_sel(same.astype(F32), log_a)
    q_in = q * jnp.exp(bcum)
    k_out = k * jnp.exp(btot - bcum)
    v_t = v.T

    row_chunk = _iota((blk, B_KW), 0) >> shift
    t_in = _iota((ch, B_KW), 0)
    head_expand = _div(_iota((B_KW, B_VW), 0), B_DK) == _div(_iota((B_KW, B_VW), 1), B_DV)
    head_expand = head_expand.astype(BF16)
    state_mask = _div(_iota((B_VW, B_KW), 0), B_DV) == _div(_iota((B_VW, B_KW), 1), B_DK)

    st = st_ref[...]
    outs = [None] * n_ch
    for ci in (range(n_ch - 1, -1, -1) if reverse else range(n_ch)):
        lo = ci * ch
        q_c, k_c, b_c, v_c = q[lo:lo + ch], k[lo:lo + ch], bcum[lo:lo + ch], v[lo:lo + ch]
        pieces = []
        for s in range(ch):
            seen = (t_in <= s) if reverse else (t_in >= s)
            arg = jnp.where(seen, b_c - b_c[s:s + 1], -jnp.inf)
            pieces.append(q_c * k_c[s:s + 1] * jnp.exp(arg))
        w = _dot(jnp.concatenate(pieces, axis=0), head_expand)
        o_c = _dot_nt(q_in[lo:lo + ch], st)
        for s in range(ch):
            o_c = o_c + w[s * ch:(s + 1) * ch] * v_c[s:s + 1]
        outs[ci] = o_c
        st_loc = _dot(v_t, jnp.where(row_chunk == ci, k_out, 0.0))
        st = st * jnp.exp(btot[lo:lo + 1]) + jnp.where(state_mask, st_loc, 0.0)
    st_ref[...] = st
    o_ref[...] = jnp.concatenate(outs, axis=0)


def _gla(pb, ps, wa_pad, ba, batch, seq, ctx_len, layer, reverse):
    n_tok = pb.shape[0]
    blk = SEQ_BLOCK
    ctx_blocks, seq_blocks = ctx_len // blk, seq // blk
    dirn = 1 if reverse else 0

    def blk_map(b, j):
        return (_seq_block(b, j, reverse, batch, ctx_blocks, seq_blocks), 0)

    return pl.pallas_call(
        functools.partial(_gla_kernel, reverse=reverse),
        out_shape=jax.ShapeDtypeStruct((n_tok, B_VW), F32),
        grid=(batch, ctx_blocks + seq_blocks),
        in_specs=[
            pl.BlockSpec((blk, 2 * B_KW + B_VW), blk_map),
            pl.BlockSpec((blk, PS_W), blk_map),
            pl.BlockSpec((None, None, PS_W, B_KW), lambda b, j: (layer, dirn, 0, 0)),
            pl.BlockSpec((None, None, 1, B_KW), lambda b, j: (layer, dirn, 0, 0)),
        ],
        out_specs=pl.BlockSpec((blk, B_VW), blk_map),
        scratch_shapes=[pltpu.VMEM((B_VW, B_KW), F32)],
        compiler_params=_params("parallel", "arbitrary"),
        name="gla_bwd" if reverse else "gla_fwd",
    )(pb, ps, wa_pad, ba)


def _by_head(cols, width):
    rows = cols[0].shape[0]
    head = _div(_iota((rows, len(cols) * width), 1), width)
    out = jnp.zeros((rows, len(cols) * width), F32)
    for h, col in enumerate(cols):
        out = jnp.where(head == h, col, out)
    return out


def _mlstm_kernel(pc_ref, prev_ref, next_ref, ps_ref, cw_ref, sel_ref, fb_ref, fbt_ref,
                  o_ref, c_ref, n_ref, m_ref, *, ctx_blocks, seq_blocks):
    d = pl.program_id(1)
    j = pl.program_id(2)

    @pl.when(j == 0)
    def _():
        c_ref[...] = jnp.zeros_like(c_ref)
        n_ref[...] = jnp.zeros_like(n_ref)
        m_ref[...] = jnp.zeros_like(m_ref)

    blk = SEQ_BLOCK
    nh, dh = C_HEADS, C_DH
    in_ctx = j < ctx_blocks
    n_seq = jnp.where(in_ctx, ctx_blocks, seq_blocks)
    jj = jnp.where(in_ctx, j, j - ctx_blocks)
    pos = jnp.where(d == 0, jj, n_seq - 1 - jj)

    x = pc_ref[:, 0:2 * C_W]
    prev_row = jnp.where(pos > 0, prev_ref[SUBLANES - 1:SUBLANES, :], 0.0)
    next_row = jnp.where(pos < n_seq - 1, next_ref[0:1, :], 0.0)
    row = _iota(x.shape, 0)
    x_prev = jnp.where(row == 0, prev_row, pltpu.roll(x, 1, 0))
    x_next = jnp.where(row == blk - 1, next_row, pltpu.roll(x, blk - 1, 0))
    qk = _silu(x_prev * cw_ref[0:1, :] + x * cw_ref[1:2, :] + x_next * cw_ref[2:3, :])
    q = qk[:, 0:C_W]
    k = qk[:, C_W:2 * C_W] * (C_DH ** -0.5)
    v = pc_ref[:, 2 * C_W:3 * C_W]

    ps = ps_ref[...]
    sel = sel_ref[...]
    g_col = _dot_sel_nt(ps, sel) + fb_ref[...]
    g_row = _sel_dot_nt(sel, ps) + jnp.concatenate([fbt_ref[...]] * (blk // LANES), axis=1)
    lf_col = _log_sigmoid(g_col)
    lf_row = _log_sigmoid(g_row)
    r = _iota((blk, blk), 0)
    c = _iota((blk, blk), 1)
    causal = jnp.where(d == 0, c, r) <= jnp.where(d == 0, r, c)
    tri = causal.astype(F32)
    b_col = _dot_sel(tri, lf_col)
    b_row = _dot_sel_nt(lf_row, tri)
    b_tot = jnp.sum(lf_row, axis=1, keepdims=True)
    lane = _iota((blk, LANES), 1)

    def column(x2, idx):
        return jnp.sum(jnp.where(lane == idx, x2, 0.0), axis=1, keepdims=True)

    head_of_lane = _div(_iota((blk, C_W), 1), dh)
    m_prev = [m_ref[h:h + 1, 0:1] for h in range(nh)]
    scores, a_cols, m_cols, w_cols, m_locs, b_tots, row_sums = [], [], [], [], [], [], []
    for h in range(nh):
        b_t = column(b_col, nh + h)
        li_t = column(g_col, h)
        dm = jnp.where(causal, b_t - b_row[nh + h:nh + h + 1, :] + g_row[h:h + 1, :], -jnp.inf)
        inter = b_t + m_prev[h]
        m_t = jnp.maximum(inter, jnp.max(dm, axis=1, keepdims=True))
        k_h = jnp.where(head_of_lane == h, k, 0.0)
        sc = _dot_nt(q, k_h) * jnp.exp(dm - m_t)
        scores.append(sc)
        row_sums.append(jnp.sum(sc, axis=1, keepdims=True))
        a_cols.append(jnp.exp(inter - m_t))
        m_cols.append(m_t)
        bt = b_tot[nh + h:nh + h + 1, :]
        g_end = bt - b_t + li_t
        m_loc = jnp.max(g_end, axis=0, keepdims=True)
        w_cols.append(jnp.exp(g_end - m_loc))
        m_locs.append(m_loc)
        b_tots.append(bt)

    a_all = _by_head(a_cols, dh)
    c_t = c_ref[...]
    n_row = n_ref[0:1, :]
    num = _dot(q * a_all, c_t)
    for h in range(nh):
        num = num + _dot(scores[h], jnp.where(head_of_lane == h, v, 0.0))
    seg = (_div(_iota((C_W, LANES), 0), dh) == _iota((C_W, LANES), 1)).astype(BF16)
    qn = _dot(q * n_row, seg)
    den_cols = [a_cols[h] * column(qn, h) + row_sums[h] for h in range(nh)]
    den_all = _by_head(den_cols, dh)
    floor_all = _by_head([jnp.exp(-m_cols[h]) for h in range(nh)], dh)
    o_ref[...] = num / jnp.maximum(jnp.abs(den_all), floor_all)

    w_all = _by_head(w_cols, dh)
    c_loc = _dot(k.T, v * w_all)
    n_loc = jnp.sum(k * w_all, axis=0, keepdims=True)
    a_st, e_st, m_new = [], [], []
    for h in range(nh):
        mn = jnp.maximum(b_tots[h] + m_prev[h], m_locs[h])
        a_st.append(jnp.exp(b_tots[h] + m_prev[h] - mn))
        e_st.append(jnp.exp(m_locs[h] - mn))
        m_new.append(mn)
    a_lane = _by_head(a_st, dh)
    e_lane = _by_head(e_st, dh)
    diag = _div(_iota((C_W, C_W), 0), dh) == _div(_iota((C_W, C_W), 1), dh)
    c_ref[...] = a_lane * c_t + jnp.where(diag, e_lane * c_loc, 0.0)
    n_ref[0:1, :] = a_lane * n_row + e_lane * n_loc
    for h in range(nh):
        m_ref[h:h + 1, :] = jnp.broadcast_to(m_new[h], (1, LANES))


def _mlstm(pc, ps, conv_w, gate_sel, f_bias_row, f_bias_col, batch, seq, ctx_len, layer):
    n_tok = pc.shape[0]
    blk = SEQ_BLOCK
    ctx_blocks, seq_blocks = ctx_len // blk, seq // blk
    rows8 = blk // SUBLANES
    last8 = n_tok // SUBLANES - 1

    def blk_of(b, d, j):
        return _seq_block(b, j, d == 1, batch, ctx_blocks, seq_blocks)

    return pl.pallas_call(
        functools.partial(_mlstm_kernel, ctx_blocks=ctx_blocks, seq_blocks=seq_blocks),
        out_shape=jax.ShapeDtypeStruct((2, n_tok, C_W), F32),
        grid=(batch, 2, ctx_blocks + seq_blocks),
        in_specs=[
            pl.BlockSpec((blk, 3 * C_W), lambda b, d, j: (blk_of(b, d, j), 0)),
            pl.BlockSpec((SUBLANES, 2 * C_W), lambda b, d, j: (jnp.maximum(blk_of(b, d, j) * rows8 - 1, 0), 0)),
            pl.BlockSpec((SUBLANES, 2 * C_W), lambda b, d, j: (jnp.minimum((blk_of(b, d, j) + 1) * rows8, last8), 0)),
            pl.BlockSpec((blk, PS_W), lambda b, d, j: (blk_of(b, d, j), 0)),
            pl.BlockSpec((None, 3, 2 * C_W), lambda b, d, j: (layer, 0, 0)),
            pl.BlockSpec((None, LANES, PS_W), lambda b, d, j: (d, 0, 0)),
            pl.BlockSpec((None, None, 1, LANES), lambda b, d, j: (layer, d, 0, 0)),
            pl.BlockSpec((None, None, LANES, LANES), lambda b, d, j: (layer, d, 0, 0)),
        ],
        out_specs=pl.BlockSpec((None, blk, C_W), lambda b, d, j: (d, blk_of(b, d, j), 0)),
        scratch_shapes=[pltpu.VMEM((C_W, C_W), F32), pltpu.VMEM((SUBLANES, C_W), F32),
                        pltpu.VMEM((SUBLANES, LANES), F32)],
        compiler_params=_params("parallel", "parallel", "arbitrary"),
        name="mlstm",
    )(pc, pc, pc, ps, conv_w, gate_sel, f_bias_row, f_bias_col)


def _head_layernorm(o, g, width):
    n = o.shape[1]
    avg = jnp.where(_div(_iota((n, n), 0), width) == _div(_iota((n, n), 1), width), 1.0 / width, 0.0)
    mu = _dot_sel_rhs(o, avg, parts=2)
    cen = o - mu
    var = _dot_sel_rhs(cen * cen, avg, parts=2)
    return cen * lax.rsqrt(var + EPS) * g


def _out_kernel(x_ref, mod_ref, g_ref, ya_ref, gb_ref, obf_ref, obb_ref, gc_ref, ocf_ref, ocb_ref,
                gng_ref, mng_ref, w_ref, o_ref):
    y_b = _silu(gb_ref[...]) * _head_layernorm(obf_ref[...] + obb_ref[...], gng_ref[...], B_DV)
    y_c = _sigmoid(gc_ref[...]) * _head_layernorm(ocf_ref[...] + ocb_ref[...], mng_ref[...], C_DH)
    y = (_dot(ya_ref[...], w_ref[0:A_OUT, :])
         + _dot(y_b, w_ref[A_OUT:A_OUT + B_VW, :])
         + _dot(y_c, w_ref[A_OUT + B_VW:, :]))
    o_ref[...] = x_ref[...] + mod_ref[5:6, :] * _rms(y, g_ref[3:4, :])


def _out(tok, modtab, norm_g, ya, pb, ob_f, ob_b, pc, oc, gla_norm, mlstm_norm, w_out, geom, layer):
    tm = geom.tm
    n_tok, d = tok.shape
    gb_col = (2 * B_KW + B_VW) // B_VW
    gc_col = 3
    return pl.pallas_call(
        _out_kernel,
        out_shape=jax.ShapeDtypeStruct((n_tok, d), F32),
        grid=(geom.tiles,),
        in_specs=[
            pl.BlockSpec((tm, d), lambda i: (i, 0)),
            pl.BlockSpec((None, None, N_MOD, d), lambda i: (layer, geom.mod_row(i), 0, 0)),
            pl.BlockSpec((None, 6, d), lambda i: (layer, 0, 0)),
            pl.BlockSpec((tm, A_OUT), lambda i: (i, 0)),
            pl.BlockSpec((tm, B_VW), lambda i: (i, gb_col)),
            pl.BlockSpec((tm, B_VW), lambda i: (i, 0)),
            pl.BlockSpec((tm, B_VW), lambda i: (i, 0)),
            pl.BlockSpec((tm, C_W), lambda i: (i, gc_col)),
            pl.BlockSpec((None, tm, C_W), lambda i: (0, i, 0)),
            pl.BlockSpec((None, tm, C_W), lambda i: (1, i, 0)),
            pl.BlockSpec((None, 1, B_VW), lambda i: (layer, 0, 0)),
            pl.BlockSpec((None, 1, C_W), lambda i: (layer, 0, 0)),
            pl.BlockSpec((None, d, d), lambda i: (layer, 0, 0)),
        ],
        out_specs=pl.BlockSpec((tm, d), lambda i: (i, 0)),
        compiler_params=_params("parallel"),
        name="out",
    )(tok, modtab, norm_g, ya, pb, ob_f, ob_b, pc, oc, oc, gla_norm, mlstm_norm, w_out)


def _rope_tables(seq, tm):
    pos = jnp.arange(seq)
    half = ROPE_AXIS // 2
    inv_freq = ROPE_BASE ** (-jnp.arange(half, dtype=F32) / half)
    ang_r = (pos // GRID_W).astype(F32)[:, None] * inv_freq[None, :]
    ang_c = (pos % GRID_W).astype(F32)[:, None] * inv_freq[None, :]
    cos = jnp.concatenate([jnp.cos(ang_r)] * 2 + [jnp.cos(ang_c)] * 2, axis=1)
    sin = jnp.concatenate([-jnp.sin(ang_r), jnp.sin(ang_r), -jnp.sin(ang_c), jnp.sin(ang_c)], axis=1)
    reps = LANES // A_HEAD_DIM
    cos = jnp.concatenate([jnp.tile(cos, (1, reps)), jnp.ones((tm, LANES), F32)], axis=0)
    sin = jnp.concatenate([jnp.tile(sin, (1, reps)), jnp.zeros((tm, LANES), F32)], axis=0)
    return cos, sin


def _regroup_in(w):
    a_end = PA_W + PB_W
    alpha = slice(a_end, a_end + 2 * GLA_RANK)
    c_qkv = slice(alpha.stop, alpha.stop + 3 * C_W)
    c_gate = slice(c_qkv.stop, c_qkv.stop + 4 * C_HEADS)
    c_og = slice(c_gate.stop, c_gate.stop + C_W)
    pad = jnp.zeros(w.shape[:-1] + (PS_W - 2 * GLA_RANK - 4 * C_HEADS,), w.dtype)
    return jnp.concatenate([w[..., :a_end], w[..., c_qkv], w[..., c_og], w[..., alpha], w[..., c_gate], pad],
                           axis=-1)


def _pick_tile(n_ctx, seq):
    for tm in (512, 256, 128):
        if n_ctx % tm == 0 and seq % tm == 0:
            return tm
    raise ValueError("token counts must be multiples of 128")


def kernel(x, c, ctx, c_ctx, ada_w, ada_b, norm_g, ffn1_wg, ffn1_wu, ffn1_wd, ffn2_wg, ffn2_wu, ffn2_wd,
           w_in, b_in, w_out, attn_sink, gla_w_alpha, gla_b_alpha, gla_norm, mlstm_conv, mlstm_f_bias,
           mlstm_norm):
    batch, seq, d = x.shape
    ctx_len = ctx.shape[1]
    n_l = ada_w.shape[0]
    assert seq % SEQ_BLOCK == 0 and ctx_len % SEQ_BLOCK == 0
    geom = _Geom(batch, seq, ctx_len, _pick_tile(batch * ctx_len, seq))

    rows = -(-(batch + 1) // SUBLANES) * SUBLANES
    cc = jnp.concatenate([c, c_ctx[None, :], jnp.zeros((rows - batch - 1, d), F32)], axis=0)
    modtab = _ada_table(cc, ada_w, ada_b)

    rope_cos, rope_sin = _rope_tables(seq, geom.tm)
    w_in_r = _regroup_in(w_in).astype(BF16)
    b_in_r = _regroup_in(b_in)[:, None, :]
    wg1, wu1, wd1 = ffn1_wg.astype(BF16), ffn1_wu.astype(BF16), ffn1_wd.astype(BF16)
    wg2, wu2, wd2 = ffn2_wg.astype(BF16), ffn2_wu.astype(BF16), ffn2_wd.astype(BF16)
    w_out_b = w_out.astype(BF16)

    wa_pad = jnp.zeros((n_l, 2, PS_W, B_KW), F32)
    for dirn in range(2):
        wa_pad = wa_pad.at[:, dirn, dirn * GLA_RANK:(dirn + 1) * GLA_RANK, :].set(gla_w_alpha[:, dirn])
    wa_pad = wa_pad.astype(BF16)
    ba = gla_b_alpha[:, :, None, :]

    nh = C_HEADS
    out_idx = jnp.arange(LANES)[:, None]
    src_idx = jnp.arange(PS_W)[None, :]
    gate_sel = jnp.stack([
        (((out_idx < nh) & (src_idx == PS_GATE0 + 2 * nh * dirn + out_idx))
         | ((out_idx >= nh) & (out_idx < 2 * nh) & (src_idx == PS_GATE0 + 2 * nh * dirn + out_idx)))
        for dirn in range(2)]).astype(F32)
    fb = jnp.zeros((n_l, 2, LANES), F32).at[:, :, nh:2 * nh].set(mlstm_f_bias)
    f_bias_row = fb[:, :, None, :]
    f_bias_col = jnp.broadcast_to(fb[:, :, :, None], (n_l, 2, LANES, LANES))

    sink_rows = jnp.broadcast_to(attn_sink[:, :, None], (n_l, A_HEADS, LANES))
    gla_g = gla_norm[:, None, :]
    mlstm_g = mlstm_norm[:, None, :]

    tok = jnp.concatenate([ctx.reshape(batch * ctx_len, d), x.reshape(batch * seq, d)], axis=0)
    for layer in range(n_l):
        last = layer == n_l - 1
        tok = _ffn(tok, modtab, norm_g, wg1, wu1, wd1, geom, layer, 0, 0)
        pa, pb, pc, ps = _inproj(tok, modtab, norm_g, w_in_r, b_in_r, rope_cos, rope_sin, geom, layer)
        ya = _attn(pa, sink_rows[layer], batch, seq, ctx_len)
        ob_f = _gla(pb, ps, wa_pad, ba, batch, seq, ctx_len, layer, False)
        ob_b = _gla(pb, ps, wa_pad, ba, batch, seq, ctx_len, layer, True)
        oc = _mlstm(pc, ps, mlstm_conv, gate_sel, f_bias_row, f_bias_col, batch, seq, ctx_len, layer)
        tok = _out(tok, modtab, norm_g, ya, pb, ob_f, ob_b, pc, oc, gla_g, mlstm_g, w_out_b, geom, layer)
        tok = _ffn(tok, modtab, norm_g, wg2, wu2, wd2, geom, layer, 6, 4, latent_only=last)
    return tok.reshape(batch, seq, d)
```

```python
import functools

import jax
import jax.numpy as jnp
from jax import lax
from jax.experimental import pallas as pl
from jax.experimental.pallas import tpu as pltpu

F32 = jnp.float32
BF16 = jnp.bfloat16

GRID_W = 64
A_HEAD_DIM = 64
A_HEADS = 8
A_KV_HEADS = 2
A_REP = A_HEADS // A_KV_HEADS
WINDOW = 128
ROPE_BASE = 10000.0
ROPE_AXIS = A_HEAD_DIM // 2
B_DK = 32
B_DV = 64
B_HEADS = 4
GLA_RANK = 16
GLA_TAU = 16.0
GLA_CHUNK = 16
C_DH = 64
C_HEADS = 4
A_OUT = A_HEADS * A_HEAD_DIM
A_KV = A_KV_HEADS * A_HEAD_DIM
B_KW = B_HEADS * B_DK
B_VW = B_HEADS * B_DV
C_W = C_HEADS * C_DH
FFN_RES = 0.5
N_MOD = 9
EPS = 1e-6

PA_W = A_OUT + 2 * A_KV
PB_W = 2 * B_KW + 2 * B_VW
PC_W = 4 * C_W
PS_W = 128
PS_GATE0 = 2 * GLA_RANK

LANES = 128
SUBLANES = 8
SEQ_BLOCK = 256
ATT_BLOCK = WINDOW
ATT_QUERIES = 2 * ATT_BLOCK
FF_CHUNK = 256
TOKEN_TILE = 512
FFN_TOKEN_TILE = 1024
V7X_VMEM_BYTES = 64 * 1024 * 1024
VMEM_LIMIT = V7X_VMEM_BYTES * 7 // 8


def _sigmoid(x):
    return 1.0 / (1.0 + jnp.exp(-x))


def _silu(x):
    return x * _sigmoid(x)


def _log_sigmoid(x):
    return jnp.minimum(x, 0.0) - jnp.log(1.0 + jnp.exp(-jnp.abs(x)))


def _rms(x, g):
    return x * lax.rsqrt(jnp.mean(x * x, axis=-1, keepdims=True) + EPS) * g


def _dot(a, b):
    return jnp.dot(a.astype(BF16), b.astype(BF16), preferred_element_type=F32)


def _dot_nt(a, b):
    return lax.dot_general(a.astype(BF16), b.astype(BF16), (((1,), (1,)), ((), ())),
                           preferred_element_type=F32)


def _split(x, parts):
    out = []
    for _ in range(parts):
        t = x.astype(BF16)
        out.append(t)
        x = x - t.astype(F32)
    return out


def _dot_sel(sel, x, parts=3):
    sel = sel.astype(BF16)
    acc = None
    for t in _split(x, parts):
        y = jnp.dot(sel, t, preferred_element_type=F32)
        acc = y if acc is None else acc + y
    return acc


def _dot_sel_rhs(x, sel, parts=3):
    sel = sel.astype(BF16)
    acc = None
    for t in _split(x, parts):
        y = jnp.dot(t, sel, preferred_element_type=F32)
        acc = y if acc is None else acc + y
    return acc


def _dot_sel_nt(x, sel, parts=3):
    sel = sel.astype(BF16)
    acc = None
    for t in _split(x, parts):
        y = lax.dot_general(t, sel, (((1,), (1,)), ((), ())), preferred_element_type=F32)
        acc = y if acc is None else acc + y
    return acc


def _sel_dot_nt(sel, x, parts=3):
    sel = sel.astype(BF16)
    acc = None
    for t in _split(x, parts):
        y = lax.dot_general(sel, t, (((1,), (1,)), ((), ())), preferred_element_type=F32)
        acc = y if acc is None else acc + y
    return acc


def _iota(shape, dim):
    return lax.broadcasted_iota(jnp.int32, shape, dim)


def _div(x, n):
    assert n & (n - 1) == 0
    return x >> (n.bit_length() - 1)


def _params(*semantics):
    return pltpu.CompilerParams(dimension_semantics=semantics, vmem_limit_bytes=VMEM_LIMIT)


def _ada_kernel(c_ref, w_ref, b_ref, o_ref):
    o_ref[...] = _dot(_silu(c_ref[...]), w_ref[...]) + b_ref[...]


def _ada_table(cc, ada_w, ada_b):
    n_l, d, _ = ada_w.shape
    r = cc.shape[0]
    out = pl.pallas_call(
        _ada_kernel,
        out_shape=jax.ShapeDtypeStruct((n_l, N_MOD, r, d), F32),
        grid=(n_l, N_MOD),
        in_specs=[
            pl.BlockSpec((r, d), lambda l, j: (0, 0)),
            pl.BlockSpec((None, d, d), lambda l, j: (l, 0, j)),
            pl.BlockSpec((None, 1, d), lambda l, j: (l, 0, j)),
        ],
        out_specs=pl.BlockSpec((None, None, r, d), lambda l, j: (l, j, 0, 0)),
        compiler_params=_params("parallel", "parallel"),
        name="ada",
    )(cc, ada_w, ada_b.reshape(n_l, 1, N_MOD * d))
    return out.transpose(0, 2, 1, 3)


class _Geom:
    def __init__(self, batch, seq, ctx_len, tm):
        self.batch, self.seq, self.ctx_len, self.tm = batch, seq, ctx_len, tm
        self.n_ctx = batch * ctx_len
        self.n_tok = batch * (ctx_len + seq)
        assert self.n_ctx % tm == 0 and seq % tm == 0
        self.ctx_tiles = self.n_ctx // tm
        self.tiles_per_batch = seq // tm
        self.tiles = self.n_tok // tm

    def mod_row(self, i):
        return jnp.where(i < self.ctx_tiles, self.batch, (i - self.ctx_tiles) // self.tiles_per_batch)

    def rope_block(self, i):
        return jnp.where(i < self.ctx_tiles, self.tiles_per_batch,
                         (i - self.ctx_tiles) % self.tiles_per_batch)


def _ffn_kernel(x_ref, mod_ref, g_ref, wg_ref, wu_ref, wd_ref, o_ref, *, mod_base, g_base):
    x = x_ref[...]
    mod = mod_ref[...]
    g = g_ref[...]
    shift = mod[mod_base:mod_base + 1]
    scale = mod[mod_base + 1:mod_base + 2]
    gate = mod[mod_base + 2:mod_base + 3]
    u = (_rms(x, g[g_base:g_base + 1]) * (1.0 + scale) + shift).astype(BF16)
    d_ff = wg_ref.shape[1]
    y = None
    for c0 in range(0, d_ff, FF_CHUNK):
        a = jnp.dot(u, wg_ref[:, c0:c0 + FF_CHUNK], preferred_element_type=F32)
        b = jnp.dot(u, wu_ref[:, c0:c0 + FF_CHUNK], preferred_element_type=F32)
        hcol = (_silu(a) * b).astype(BF16)
        part = jnp.dot(hcol, wd_ref[c0:c0 + FF_CHUNK, :], preferred_element_type=F32)
        y = part if y is None else y + part
    o_ref[...] = x + FFN_RES * gate * _rms(y, g[g_base + 1:g_base + 2])


def _ffn(tok, modtab, norm_g, wg, wu, wd, geom, layer, mod_base, g_base, latent_only=False):
    tm = geom.tm
    d = tok.shape[1]
    d_ff = wg.shape[2]
    assert d_ff % FF_CHUNK == 0
    first = geom.ctx_tiles if latent_only else 0
    n_tiles = geom.tiles - first
    return pl.pallas_call(
        functools.partial(_ffn_kernel, mod_base=mod_base, g_base=g_base),
        out_shape=jax.ShapeDtypeStruct((n_tiles * tm, d), F32),
        grid=(n_tiles,),
        in_specs=[
            pl.BlockSpec((tm, d), lambda i: (i + first, 0)),
            pl.BlockSpec((None, None, N_MOD, d), lambda i: (layer, geom.mod_row(i + first), 0, 0)),
            pl.BlockSpec((None, 6, d), lambda i: (layer, 0, 0)),
            pl.BlockSpec((None, d, d_ff), lambda i: (layer, 0, 0), pipeline_mode=pl.Buffered(1)),
            pl.BlockSpec((None, d, d_ff), lambda i: (layer, 0, 0), pipeline_mode=pl.Buffered(1)),
            pl.BlockSpec((None, d_ff, d), lambda i: (layer, 0, 0), pipeline_mode=pl.Buffered(1)),
        ],
        out_specs=pl.BlockSpec((tm, d), lambda i: (i, 0)),
        compiler_params=_params("parallel"),
        name="ffn",
    )(tok, modtab, norm_g, wg, wu, wd)


def _inproj_kernel(x_ref, mod_ref, g_ref, w_ref, b_ref, cos_ref, sin_ref,
                   pa_ref, pb_ref, pc_ref, ps_ref):
    x = x_ref[...]
    mod = mod_ref[...]
    g = g_ref[...]
    u = _rms(x, g[2:3]) * (1.0 + mod[4:5]) + mod[3:4]
    p = _dot(u, w_ref[...]) + b_ref[...]
    qk_w = A_OUT + A_KV
    qk = p[:, 0:qk_w]
    reps = qk_w // LANES
    cos = jnp.concatenate([cos_ref[...]] * reps, axis=1)
    sin = jnp.concatenate([sin_ref[...]] * reps, axis=1)
    half = ROPE_AXIS // 2
    first_half = (_iota(qk.shape, 1) & (ROPE_AXIS - 1)) < half
    partner = jnp.where(first_half, pltpu.roll(qk, qk_w - half, 1), pltpu.roll(qk, half, 1))
    rot = qk * cos + partner * sin
    pa_ref[:, 0:A_OUT] = rot[:, 0:A_OUT] * (A_HEAD_DIM ** -0.5)
    pa_ref[:, A_OUT:qk_w] = rot[:, A_OUT:qk_w]
    pa_ref[:, qk_w:PA_W] = p[:, qk_w:PA_W]
    pb_ref[...] = p[:, PA_W:PA_W + PB_W]
    pc_ref[...] = p[:, PA_W + PB_W:PA_W + PB_W + PC_W]
    ps_ref[...] = p[:, PA_W + PB_W + PC_W:]


def _inproj(tok, modtab, norm_g, w_in, b_in, rope_cos, rope_sin, geom, layer):
    tm = geom.tm
    d = tok.shape[1]
    n_tok = tok.shape[0]
    width = w_in.shape[2]
    outs = [jax.ShapeDtypeStruct((n_tok, w), F32) for w in (PA_W, PB_W, PC_W, PS_W)]
    return pl.pallas_call(
        _inproj_kernel,
        out_shape=outs,
        grid=(geom.tiles,),
        in_specs=[
            pl.BlockSpec((tm, d), lambda i: (i, 0)),
            pl.BlockSpec((None, None, N_MOD, d), lambda i: (layer, geom.mod_row(i), 0, 0)),
            pl.BlockSpec((None, 6, d), lambda i: (layer, 0, 0)),
            pl.BlockSpec((None, d, width), lambda i: (layer, 0, 0)),
            pl.BlockSpec((None, 1, width), lambda i: (layer, 0, 0)),
            pl.BlockSpec((tm, LANES), lambda i: (geom.rope_block(i), 0)),
            pl.BlockSpec((tm, LANES), lambda i: (geom.rope_block(i), 0)),
        ],
        out_specs=[pl.BlockSpec((tm, w), lambda i: (i, 0)) for w in (PA_W, PB_W, PC_W, PS_W)],
        compiler_params=_params("parallel"),
        name="inproj",
    )(tok, modtab, norm_g, w_in, b_in, rope_cos, rope_sin)


def _attn_kernel(q_ref, k0_ref, k1_ref, k2_ref, k3_ref, v0_ref, v1_ref, v2_ref, v3_ref, kc_ref, vc_ref,
                 sink_ref, o_ref, *, ctx_steps, seq_len):
    n = pl.program_id(1) - ctx_steps
    qb = ATT_QUERIES
    n_win = 4 * ATT_BLOCK
    hd = A_HEAD_DIM
    k_abs = n * qb - WINDOW + _iota((n_win, qb), 0)
    q_abs = n * qb + _iota((n_win, qb), 1)
    hi = jnp.where(n >= 0, seq_len, 0)
    mask = (jnp.abs(q_abs - k_abs) <= WINDOW) & (k_abs >= 0) & (k_abs < hi)
    q_t = q_ref[...].T
    k_all = jnp.concatenate([k0_ref[...], k1_ref[...], k2_ref[...], k3_ref[...], kc_ref[...]], axis=0)
    v_t = jnp.concatenate([v0_ref[...], v1_ref[...], v2_ref[...], v3_ref[...], vc_ref[...]], axis=0).T
    zero = jnp.zeros((hd, qb), F32)
    cols, sinks = [], []
    for h in range(A_HEADS):
        q_h = q_t[h * hd:(h + 1) * hd]
        cols.append(jnp.concatenate([q_h, zero] if h // A_REP == 0 else [zero, q_h], axis=0))
        sinks.append(jnp.broadcast_to(sink_ref[h:h + 1, 0:1], (1, qb)))
    sk = jnp.concatenate(sinks, axis=1)
    s = _dot(k_all, jnp.concatenate(cols, axis=1))
    s_win = jnp.where(jnp.concatenate([mask] * A_HEADS, axis=1), s[:n_win], -jnp.inf)
    s_ctx = s[n_win:]
    m = jnp.maximum(jnp.maximum(jnp.max(s_win, axis=0, keepdims=True),
                                jnp.max(s_ctx, axis=0, keepdims=True)), sk)
    e_win = jnp.exp(s_win - m)
    e_ctx = jnp.exp(s_ctx - m)
    den = (jnp.sum(e_win, axis=0, keepdims=True) + jnp.sum(e_ctx, axis=0, keepdims=True)
           + jnp.exp(sk - m))
    p = jnp.concatenate([e_win, e_ctx], axis=0).astype(BF16)
    per_group = A_REP * qb
    outs = []
    for g in range(A_KV_HEADS):
        lanes = slice(g * per_group, (g + 1) * per_group)
        o_g = _dot(v_t[g * hd:(g + 1) * hd], p[:, lanes]) / den[:, lanes]
        outs += [o_g[:, r * qb:(r + 1) * qb] for r in range(A_REP)]
    o_ref[...] = jnp.concatenate(outs, axis=0).T


def _attn(pa, sink_rows, batch, seq, ctx_len):
    n_tok = pa.shape[0]
    qb = ATT_QUERIES
    blk = ATT_BLOCK
    per_q = qb // blk
    ctx_steps = ctx_len // qb
    seq_steps = seq // qb
    seq_blocks = seq // blk
    lat0 = batch * (ctx_len // blk)
    k_col = A_OUT // LANES
    v_col = k_col + 1

    def q_map(b, j):
        return (jnp.where(j < ctx_steps, b * ctx_steps + j, batch * ctx_steps + b * seq_steps + j - ctx_steps), 0)

    def kv_map(i, col):
        def f(b, j):
            n = jnp.clip((j - ctx_steps) * per_q - 1 + i, 0, seq_blocks - 1)
            return (lat0 + b * seq_blocks + n, col)
        return f

    kv_specs = [pl.BlockSpec((blk, LANES), kv_map(i, col))
                for col in (k_col, v_col) for i in range(per_q + 2)]
    return pl.pallas_call(
        functools.partial(_attn_kernel, ctx_steps=ctx_steps, seq_len=seq),
        out_shape=jax.ShapeDtypeStruct((n_tok, A_OUT), F32),
        grid=(batch, ctx_steps + seq_steps),
        in_specs=[pl.BlockSpec((qb, A_OUT), q_map)] + kv_specs + [
            pl.BlockSpec((ctx_len, LANES), lambda b, j: (b, k_col)),
            pl.BlockSpec((ctx_len, LANES), lambda b, j: (b, v_col)),
            pl.BlockSpec((SUBLANES, LANES), lambda b, j: (0, 0)),
        ],
        out_specs=pl.BlockSpec((qb, A_OUT), q_map),
        compiler_params=_params("parallel", "parallel"),
        name="attn",
    )(*([pa] * (2 * (per_q + 2) + 3)), sink_rows)


def _seq_block(b, j, reverse, batch, ctx_blocks, seq_blocks):
    jc = jnp.where(reverse, ctx_blocks - 1 - j, j)
    jl = jnp.where(reverse, seq_blocks - 1 - (j - ctx_blocks), j - ctx_blocks)
    return jnp.where(j < ctx_blocks, b * ctx_blocks + jc, batch * ctx_blocks + b * seq_blocks + jl)


def _gla_kernel(pb_ref, ps_ref, wa_ref, ba_ref, o_ref, st_ref, *, reverse):
    @pl.when(pl.program_id(1) == 0)
    def _():
        st_ref[...] = jnp.zeros_like(st_ref)

    blk = SEQ_BLOCK
    ch = GLA_CHUNK
    n_ch = blk // ch
    shift = ch.bit_length() - 1
    q = pb_ref[:, 0:B_KW] * (B_DK ** -0.5)
    k = pb_ref[:, B_KW:2 * B_KW]
    v = pb_ref[:, 2 * B_KW:2 * B_KW + B_VW]
    log_a = _log_sigmoid(_dot(ps_ref[...], wa_ref[...]) + ba_ref[...]) * (1.0 / GLA_TAU)

    r = _iota((blk, blk), 0)
    c = _iota((blk, blk), 1)
    same = (r >> shift) == (c >> shift)
    tri = same & ((c >= r) if reverse else (c <= r))
    bcum = _dot_sel(tri.astype(F32), log_a)
    btot = _dot_sel(same.astype(F32), log_a)
    q_in = q * jnp.exp(bcum)
    k_out = k * jnp.exp(btot - bcum)
    v_t = v.T

    row_chunk = _iota((blk, B_KW), 0) >> shift
    t_in = _iota((ch, B_KW), 0)
    head_expand = _div(_iota((B_KW, B_VW), 0), B_DK) == _div(_iota((B_KW, B_VW), 1), B_DV)
    head_expand = head_expand.astype(BF16)
    state_mask = _div(_iota((B_VW, B_KW), 0), B_DV) == _div(_iota((B_VW, B_KW), 1), B_DK)

    st = st_ref[...]
    outs = [None] * n_ch
    for ci in (range(n_ch - 1, -1, -1) if reverse else range(n_ch)):
        lo = ci * ch
        q_c, k_c, b_c, v_c = q[lo:lo + ch], k[lo:lo + ch], bcum[lo:lo + ch], v[lo:lo + ch]
        pieces = []
        for s in range(ch):
            seen = (t_in <= s) if reverse else (t_in >= s)
            arg = jnp.where(seen, b_c - b_c[s:s + 1], -jnp.inf)
            pieces.append(q_c * k_c[s:s + 1] * jnp.exp(arg))
        w = _dot(jnp.concatenate(pieces, axis=0), head_expand)
        o_c = _dot_nt(q_in[lo:lo + ch], st)
        for s in range(ch):
            o_c = o_c + w[s * ch:(s + 1) * ch] * v_c[s:s + 1]
        outs[ci] = o_c
        st_loc = _dot(v_t, jnp.where(row_chunk == ci, k_out, 0.0))
        st = st * jnp.exp(btot[lo:lo + 1]) + jnp.where(state_mask, st_loc, 0.0)
    st_ref[...] = st
    o_ref[...] = jnp.concatenate(outs, axis=0)


def _gla(pb, ps, wa_pad, ba, batch, seq, ctx_len, layer, reverse):
    n_tok = pb.shape[0]
    blk = SEQ_BLOCK
    ctx_blocks, seq_blocks = ctx_len // blk, seq // blk
    dirn = 1 if reverse else 0

    def blk_map(b, j):
        return (_seq_block(b, j, reverse, batch, ctx_blocks, seq_blocks), 0)

    return pl.pallas_call(
        functools.partial(_gla_kernel, reverse=reverse),
        out_shape=jax.ShapeDtypeStruct((n_tok, B_VW), F32),
        grid=(batch, ctx_blocks + seq_blocks),
        in_specs=[
            pl.BlockSpec((blk, 2 * B_KW + B_VW), blk_map),
            pl.BlockSpec((blk, PS_W), blk_map),
            pl.BlockSpec((None, None, PS_W, B_KW), lambda b, j: (layer, dirn, 0, 0)),
            pl.BlockSpec((None, None, 1, B_KW), lambda b, j: (layer, dirn, 0, 0)),
        ],
        out_specs=pl.BlockSpec((blk, B_VW), blk_map),
        scratch_shapes=[pltpu.VMEM((B_VW, B_KW), F32)],
        compiler_params=_params("parallel", "arbitrary"),
        name="gla_bwd" if reverse else "gla_fwd",
    )(pb, ps, wa_pad, ba)


def _lane_scan(x, op, identity, reverse):
    n = x.shape[1]
    lane = _iota(x.shape, 1)
    k = 1
    while k < n:
        shifted = pltpu.roll(x, n - k if reverse else k, 1)
        inside = (lane < n - k) if reverse else (lane >= k)
        x = op(x, jnp.where(inside, shifted, identity))
        k *= 2
        yield
    return x


def _round_robin(*chains):
    live = list(chains)
    while live:
        for chain in list(live):
            try:
                next(chain)
            except StopIteration:
                live.remove(chain)


def _mlstm_kernel(pcf_ref, prevf_ref, nextf_ref, psf_ref, pcb_ref, prevb_ref, nextb_ref, psb_ref,
                  cw_ref, sel_ref, fb_ref, of_ref, ob_ref, c_ref, n_ref, m_ref, rows_ref, stat_ref,
                  *, ctx_blocks, seq_blocks):
    j = pl.program_id(1)

    @pl.when(j == 0)
    def _():
        c_ref[...] = jnp.zeros_like(c_ref)
        n_ref[...] = jnp.zeros_like(n_ref)
        m_ref[...] = jnp.zeros_like(m_ref)

    _round_robin(
        _mlstm_gates(psf_ref, sel_ref.at[0], fb_ref.at[0], rows_ref.at[0], stat_ref.at[0], rev=False),
        _mlstm_gates(psb_ref, sel_ref.at[1], fb_ref.at[1], rows_ref.at[1], stat_ref.at[1], rev=True))
    in_ctx = j < ctx_blocks
    n_seq = jnp.where(in_ctx, ctx_blocks, seq_blocks)
    jj = jnp.where(in_ctx, j, j - ctx_blocks)
    _round_robin(
        _mlstm_step(pcf_ref, prevf_ref, nextf_ref, cw_ref, rows_ref.at[0], stat_ref.at[0], of_ref,
                    c_ref.at[0], n_ref.at[0], m_ref.at[0], rev=False, pos=jj, n_seq=n_seq),
        _mlstm_step(pcb_ref, prevb_ref, nextb_ref, cw_ref, rows_ref.at[1], stat_ref.at[1], ob_ref,
                    c_ref.at[1], n_ref.at[1], m_ref.at[1], rev=True, pos=n_seq - 1 - jj, n_seq=n_seq))


def _mlstm_gates(ps_ref, sel_ref, fb_ref, rows_ref, stat_ref, *, rev):
    sub, blk, nh = SUBLANES, SEQ_BLOCK, C_HEADS
    g = _sel_dot_nt(sel_ref[...], ps_ref[...])
    li = g[0:sub]
    lf = jnp.where(_iota((sub, blk), 0) < nh, _log_sigmoid(g[sub:2 * sub] + fb_ref[:, 0:1]), 0.0)
    yield
    bcum = yield from _lane_scan(lf, jnp.add, 0.0, rev)
    btot = jnp.sum(lf, axis=1, keepdims=True)
    u = li - bcum
    u_max = yield from _lane_scan(u, jnp.maximum, -jnp.inf, rev)
    g_end = btot - bcum + li
    m_loc = jnp.max(g_end, axis=1, keepdims=True)
    rows_ref[...] = jnp.concatenate([u, u_max, bcum, jnp.exp(g_end - m_loc)], axis=0)
    stat_ref[...] = jnp.concatenate([jnp.broadcast_to(btot, (sub, LANES)),
                                     jnp.broadcast_to(m_loc, (sub, LANES))], axis=0)


def _mlstm_step(pc_ref, prev_ref, next_ref, cw_ref, rows_ref, stat_ref, o_ref, c_ref, n_ref, m_ref,
                *, rev, pos, n_seq):
    blk = SEQ_BLOCK
    nh, dh = C_HEADS, C_DH
    sub = SUBLANES
    u = rows_ref[0:sub, :]
    u_max = rows_ref[sub:2 * sub, :]
    bcum = rows_ref[2 * sub:3 * sub, :]
    w_end = rows_ref[3 * sub:4 * sub, :]
    btot = stat_ref[0:sub, 0:1]
    m_loc = stat_ref[sub:2 * sub, 0:1]
    yield

    x = pc_ref[:, 0:2 * C_W]
    prev_row = jnp.where(pos > 0, prev_ref[SUBLANES - 1:SUBLANES, :], 0.0)
    next_row = jnp.where(pos < n_seq - 1, next_ref[0:1, :], 0.0)
    row = _iota(x.shape, 0)
    x_prev = jnp.where(row == 0, prev_row, pltpu.roll(x, 1, 0))
    x_next = jnp.where(row == blk - 1, next_row, pltpu.roll(x, blk - 1, 0))
    qk = _silu(x_prev * cw_ref[0:1, :] + x * cw_ref[1:2, :] + x_next * cw_ref[2:3, :])
    q = qk[:, 0:C_W]
    k = qk[:, C_W:2 * C_W] * (C_DH ** -0.5)
    v = pc_ref[:, 2 * C_W:3 * C_W]

    m_prev = m_ref[:, 0:1]
    big_m = jnp.maximum(u_max, m_prev)
    a_t = jnp.exp(m_prev - big_m)
    neg_m_t = -(bcum + big_m)
    m_new = jnp.maximum(btot + m_prev, m_loc)
    a_st = jnp.exp(btot + m_prev - m_new)
    e_st = jnp.exp(m_loc - m_new)

    pad = jnp.zeros((LANES - 4 * sub, blk), F32)
    yield
    cols = jnp.concatenate([big_m, a_t, neg_m_t, w_end, pad], axis=0).T
    yield
    width = nh * blk + 3 * C_W
    src_row = _iota((LANES, width), 0)
    lane = _iota((LANES, width), 1)
    tail = lane - nh * blk
    src = jnp.where(tail < 0, _div(lane, blk), sub * (1 + _div(tail, C_W)) + _div(tail & (C_W - 1), dh))
    wide = _dot_sel_rhs(cols, (src_row == src).astype(F32))
    a_all = wide[:, nh * blk:nh * blk + C_W]
    floor_all = jnp.exp(wide[:, nh * blk + C_W:nh * blk + 2 * C_W])
    w_all = wide[:, nh * blk + 2 * C_W:]

    c_t = c_ref[...]
    n_row = n_ref[0:1, :]
    rr = _iota((blk, blk), 0)
    cc = _iota((blk, blk), 1)
    causal = (rr <= cc) if rev else (cc <= rr)
    head_of_lane = _div(_iota((blk, C_W), 1), dh)
    num = a_all * _dot(q, c_t)
    row_sum = None
    for h in range(nh):
        in_head = head_of_lane == h
        e = jnp.where(causal, u[h:h + 1, :] - wide[:, h * blk:(h + 1) * blk], -jnp.inf)
        sc = (_dot_nt(q, jnp.where(in_head, k, 0.0)) * jnp.exp(e)).astype(BF16)
        num = num + jnp.dot(sc, jnp.where(in_head, v, 0.0).astype(BF16), preferred_element_type=F32)
        rs = jnp.dot(sc, in_head.astype(BF16), preferred_element_type=F32)
        row_sum = rs if row_sum is None else row_sum + rs
        yield
    same_head = _div(_iota((C_W, C_W), 0), dh) == _div(_iota((C_W, C_W), 1), dh)
    den = a_all * _dot(q * n_row, same_head) + row_sum
    o_ref[...] = num / jnp.maximum(jnp.abs(den), floor_all)
    yield

    head8 = _div(_iota((sub, C_W), 1), dh) == _iota((sub, C_W), 0)
    a_lane = jnp.sum(jnp.where(head8, a_st, 0.0), axis=0, keepdims=True)
    e_lane = jnp.sum(jnp.where(head8, e_st, 0.0), axis=0, keepdims=True)
    c_loc = _dot(k.T, v * w_all)
    c_ref[...] = a_lane * c_t + jnp.where(same_head, e_lane * c_loc, 0.0)
    n_ref[0:1, :] = a_lane * n_row + e_lane * jnp.sum(k * w_all, axis=0, keepdims=True)
    m_ref[...] = jnp.broadcast_to(m_new, m_ref.shape)


def _mlstm(pc, ps, conv_w, gate_sel, f_bias_rows, batch, seq, ctx_len, layer):
    n_tok = pc.shape[0]
    blk = SEQ_BLOCK
    ctx_blocks, seq_blocks = ctx_len // blk, seq // blk
    rows8 = blk // SUBLANES
    last8 = n_tok // SUBLANES - 1

    def chain_specs(reverse):
        def blk_of(b, j):
            return _seq_block(b, j, reverse, batch, ctx_blocks, seq_blocks)
        return [
            pl.BlockSpec((blk, 3 * C_W), lambda b, j: (blk_of(b, j), 0)),
            pl.BlockSpec((SUBLANES, 2 * C_W), lambda b, j: (jnp.maximum(blk_of(b, j) * rows8 - 1, 0), 0)),
            pl.BlockSpec((SUBLANES, 2 * C_W), lambda b, j: (jnp.minimum((blk_of(b, j) + 1) * rows8, last8), 0)),
            pl.BlockSpec((blk, PS_W), lambda b, j: (blk_of(b, j), 0)),
        ], pl.BlockSpec((blk, C_W), lambda b, j: (blk_of(b, j), 0))

    in_f, out_f = chain_specs(False)
    in_b, out_b = chain_specs(True)
    out = jax.ShapeDtypeStruct((n_tok, C_W), F32)
    return pl.pallas_call(
        functools.partial(_mlstm_kernel, ctx_blocks=ctx_blocks, seq_blocks=seq_blocks),
        out_shape=[out, out],
        grid=(batch, ctx_blocks + seq_blocks),
        in_specs=in_f + in_b + [
            pl.BlockSpec((None, 3, 2 * C_W), lambda b, j: (layer, 0, 0)),
            pl.BlockSpec((2, 2 * SUBLANES, PS_W), lambda b, j: (0, 0, 0)),
            pl.BlockSpec((None, 2, SUBLANES, LANES), lambda b, j: (layer, 0, 0, 0)),
        ],
        out_specs=[out_f, out_b],
        scratch_shapes=[pltpu.VMEM((2, C_W, C_W), F32), pltpu.VMEM((2, SUBLANES, C_W), F32),
                        pltpu.VMEM((2, SUBLANES, LANES), F32), pltpu.VMEM((2, 4 * SUBLANES, blk), F32),
                        pltpu.VMEM((2, 2 * SUBLANES, LANES), F32)],
        compiler_params=_params("parallel", "arbitrary"),
        name="mlstm",
    )(pc, pc, pc, ps, pc, pc, pc, ps, conv_w, gate_sel, f_bias_rows)


def _head_layernorm(o, g, width):
    n = o.shape[1]
    avg = jnp.where(_div(_iota((n, n), 0), width) == _div(_iota((n, n), 1), width), 1.0 / width, 0.0)
    mu = _dot_sel_rhs(o, avg, parts=2)
    cen = o - mu
    var = _dot_sel_rhs(cen * cen, avg, parts=2)
    return cen * lax.rsqrt(var + EPS) * g


def _out_kernel(x_ref, mod_ref, g_ref, ya_ref, gb_ref, obf_ref, obb_ref, gc_ref, ocf_ref, ocb_ref,
                gng_ref, mng_ref, w_ref, o_ref):
    y_b = _silu(gb_ref[...]) * _head_layernorm(obf_ref[...] + obb_ref[...], gng_ref[...], B_DV)
    y_c = _sigmoid(gc_ref[...]) * _head_layernorm(ocf_ref[...] + ocb_ref[...], mng_ref[...], C_DH)
    y = (_dot(ya_ref[...], w_ref[0:A_OUT, :])
         + _dot(y_b, w_ref[A_OUT:A_OUT + B_VW, :])
         + _dot(y_c, w_ref[A_OUT + B_VW:, :]))
    o_ref[...] = x_ref[...] + mod_ref[5:6, :] * _rms(y, g_ref[3:4, :])


def _out(tok, modtab, norm_g, ya, pb, ob_f, ob_b, pc, oc_f, oc_b, gla_norm, mlstm_norm, w_out, geom, layer):
    tm = geom.tm
    n_tok, d = tok.shape
    gb_col = (2 * B_KW + B_VW) // B_VW
    gc_col = 3
    return pl.pallas_call(
        _out_kernel,
        out_shape=jax.ShapeDtypeStruct((n_tok, d), F32),
        grid=(geom.tiles,),
        in_specs=[
            pl.BlockSpec((tm, d), lambda i: (i, 0)),
            pl.BlockSpec((None, None, N_MOD, d), lambda i: (layer, geom.mod_row(i), 0, 0)),
            pl.BlockSpec((None, 6, d), lambda i: (layer, 0, 0)),
            pl.BlockSpec((tm, A_OUT), lambda i: (i, 0)),
            pl.BlockSpec((tm, B_VW), lambda i: (i, gb_col)),
            pl.BlockSpec((tm, B_VW), lambda i: (i, 0)),
            pl.BlockSpec((tm, B_VW), lambda i: (i, 0)),
            pl.BlockSpec((tm, C_W), lambda i: (i, gc_col)),
            pl.BlockSpec((tm, C_W), lambda i: (i, 0)),
            pl.BlockSpec((tm, C_W), lambda i: (i, 0)),
            pl.BlockSpec((None, 1, B_VW), lambda i: (layer, 0, 0)),
            pl.BlockSpec((None, 1, C_W), lambda i: (layer, 0, 0)),
            pl.BlockSpec((None, d, d), lambda i: (layer, 0, 0)),
        ],
        out_specs=pl.BlockSpec((tm, d), lambda i: (i, 0)),
        compiler_params=_params("parallel"),
        name="out",
    )(tok, modtab, norm_g, ya, pb, ob_f, ob_b, pc, oc_f, oc_b, gla_norm, mlstm_norm, w_out)


def _rope_tables(seq, tm):
    pos = jnp.arange(seq)
    half = ROPE_AXIS // 2
    inv_freq = ROPE_BASE ** (-jnp.arange(half, dtype=F32) / half)
    ang_r = (pos // GRID_W).astype(F32)[:, None] * inv_freq[None, :]
    ang_c = (pos % GRID_W).astype(F32)[:, None] * inv_freq[None, :]
    cos = jnp.concatenate([jnp.cos(ang_r)] * 2 + [jnp.cos(ang_c)] * 2, axis=1)
    sin = jnp.concatenate([-jnp.sin(ang_r), jnp.sin(ang_r), -jnp.sin(ang_c), jnp.sin(ang_c)], axis=1)
    reps = LANES // A_HEAD_DIM
    cos = jnp.concatenate([jnp.tile(cos, (1, reps)), jnp.ones((tm, LANES), F32)], axis=0)
    sin = jnp.concatenate([jnp.tile(sin, (1, reps)), jnp.zeros((tm, LANES), F32)], axis=0)
    return cos, sin


def _regroup_in(w):
    a_end = PA_W + PB_W
    alpha = slice(a_end, a_end + 2 * GLA_RANK)
    c_qkv = slice(alpha.stop, alpha.stop + 3 * C_W)
    c_gate = slice(c_qkv.stop, c_qkv.stop + 4 * C_HEADS)
    c_og = slice(c_gate.stop, c_gate.stop + C_W)
    pad = jnp.zeros(w.shape[:-1] + (PS_W - 2 * GLA_RANK - 4 * C_HEADS,), w.dtype)
    return jnp.concatenate([w[..., :a_end], w[..., c_qkv], w[..., c_og], w[..., alpha], w[..., c_gate], pad],
                           axis=-1)


def _pick_tile(n_ctx, seq, largest):
    tm = largest
    while tm >= LANES:
        if n_ctx % tm == 0 and seq % tm == 0:
            return tm
        tm //= 2
    raise ValueError("token counts must be multiples of 128")


def kernel(x, c, ctx, c_ctx, ada_w, ada_b, norm_g, ffn1_wg, ffn1_wu, ffn1_wd, ffn2_wg, ffn2_wu, ffn2_wd,
           w_in, b_in, w_out, attn_sink, gla_w_alpha, gla_b_alpha, gla_norm, mlstm_conv, mlstm_f_bias,
           mlstm_norm):
    batch, seq, d = x.shape
    ctx_len = ctx.shape[1]
    n_l = ada_w.shape[0]
    assert seq % SEQ_BLOCK == 0 and ctx_len % SEQ_BLOCK == 0
    geom = _Geom(batch, seq, ctx_len, _pick_tile(batch * ctx_len, seq, TOKEN_TILE))
    geom_ffn = _Geom(batch, seq, ctx_len, _pick_tile(batch * ctx_len, seq, FFN_TOKEN_TILE))

    rows = -(-(batch + 1) // SUBLANES) * SUBLANES
    cc = jnp.concatenate([c, c_ctx[None, :], jnp.zeros((rows - batch - 1, d), F32)], axis=0)
    modtab = _ada_table(cc, ada_w, ada_b)

    rope_cos, rope_sin = _rope_tables(seq, geom.tm)
    w_in_r = _regroup_in(w_in).astype(BF16)
    b_in_r = _regroup_in(b_in)[:, None, :]
    wg1, wu1, wd1 = ffn1_wg.astype(BF16), ffn1_wu.astype(BF16), ffn1_wd.astype(BF16)
    wg2, wu2, wd2 = ffn2_wg.astype(BF16), ffn2_wu.astype(BF16), ffn2_wd.astype(BF16)
    w_out_b = w_out.astype(BF16)

    wa_pad = jnp.zeros((n_l, 2, PS_W, B_KW), F32)
    for dirn in range(2):
        wa_pad = wa_pad.at[:, dirn, dirn * GLA_RANK:(dirn + 1) * GLA_RANK, :].set(gla_w_alpha[:, dirn])
    wa_pad = wa_pad.astype(BF16)
    ba = gla_b_alpha[:, :, None, :]

    nh = C_HEADS
    out_idx = jnp.arange(2 * SUBLANES)[:, None]
    src_idx = jnp.arange(PS_W)[None, :]
    head_idx = out_idx % SUBLANES
    gate_sel = jnp.stack([
        (head_idx < nh) & (src_idx == PS_GATE0 + 2 * nh * dirn + nh * (out_idx // SUBLANES) + head_idx)
        for dirn in range(2)]).astype(F32)
    f_bias_rows = jnp.zeros((n_l, 2, SUBLANES), F32).at[:, :, 0:nh].set(mlstm_f_bias)
    f_bias_rows = jnp.broadcast_to(f_bias_rows[:, :, :, None], (n_l, 2, SUBLANES, LANES))

    sink_rows = jnp.broadcast_to(attn_sink[:, :, None], (n_l, A_HEADS, LANES))
    gla_g = gla_norm[:, None, :]
    mlstm_g = mlstm_norm[:, None, :]

    tok = jnp.concatenate([ctx.reshape(batch * ctx_len, d), x.reshape(batch * seq, d)], axis=0)
    for layer in range(n_l):
        last = layer == n_l - 1
        tok = _ffn(tok, modtab, norm_g, wg1, wu1, wd1, geom_ffn, layer, 0, 0)
        pa, pb, pc, ps = _inproj(tok, modtab, norm_g, w_in_r, b_in_r, rope_cos, rope_sin, geom, layer)
        ya = _attn(pa, sink_rows[layer], batch, seq, ctx_len)
        ob_f = _gla(pb, ps, wa_pad, ba, batch, seq, ctx_len, layer, False)
        ob_b = _gla(pb, ps, wa_pad, ba, batch, seq, ctx_len, layer, True)
        oc_f, oc_b = _mlstm(pc, ps, mlstm_conv, gate_sel, f_bias_rows, batch, seq, ctx_len, layer)
        tok = _out(tok, modtab, norm_g, ya, pb, ob_f, ob_b, pc, oc_f, oc_b, gla_g, mlstm_g, w_out_b, geom, layer)
        tok = _ffn(tok, modtab, norm_g, wg2, wu2, wd2, geom_ffn, layer, 6, 4, latent_only=last)
    return tok.reshape(batch, seq, d)
```

```python
import functools

import jax
import jax.numpy as jnp
from jax import lax
from jax.experimental import pallas as pl
from jax.experimental.pallas import tpu as pltpu

F32 = jnp.float32
BF16 = jnp.bfloat16

GRID_W = 64
A_HEAD_DIM = 64
A_HEADS = 8
A_KV_HEADS = 2
A_REP = A_HEADS // A_KV_HEADS
WINDOW = 128
ROPE_BASE = 10000.0
ROPE_AXIS = A_HEAD_DIM // 2
B_DK = 32
B_DV = 64
B_HEADS = 4
GLA_RANK = 16
GLA_TAU = 16.0
GLA_CHUNK = 16
C_DH = 64
C_HEADS = 4
A_OUT = A_HEADS * A_HEAD_DIM
A_KV = A_KV_HEADS * A_HEAD_DIM
B_KW = B_HEADS * B_DK
B_VW = B_HEADS * B_DV
C_W = C_HEADS * C_DH
FFN_RES = 0.5
N_MOD = 9
EPS = 1e-6

PA_W = A_OUT + 2 * A_KV
PB_W = 2 * B_KW + 2 * B_VW
PC_W = 4 * C_W
PS_W = 128
PS_GATE0 = 2 * GLA_RANK

LANES = 128
SUBLANES = 8
SEQ_BLOCK = 256
ATT_BLOCK = WINDOW
ATT_QUERIES = 2 * ATT_BLOCK
FF_CHUNK = 256
TOKEN_TILE = 512
FFN_TOKEN_TILE = 1024
V7X_VMEM_BYTES = 64 * 1024 * 1024
VMEM_LIMIT = V7X_VMEM_BYTES * 7 // 8


def _sigmoid(x):
    return 1.0 / (1.0 + jnp.exp(-x))


def _silu(x):
    return x * _sigmoid(x)


def _log_sigmoid(x):
    return jnp.minimum(x, 0.0) - jnp.log(1.0 + jnp.exp(-jnp.abs(x)))


def _rms(x, g):
    return x * lax.rsqrt(jnp.mean(x * x, axis=-1, keepdims=True) + EPS) * g


def _dot(a, b):
    return jnp.dot(a.astype(BF16), b.astype(BF16), preferred_element_type=F32)


def _dot_nt(a, b):
    return lax.dot_general(a.astype(BF16), b.astype(BF16), (((1,), (1,)), ((), ())),
                           preferred_element_type=F32)


def _split(x, parts):
    out = []
    for _ in range(parts):
        t = x.astype(BF16)
        out.append(t)
        x = x - t.astype(F32)
    return out


def _dot_sel(sel, x, parts=3):
    sel = sel.astype(BF16)
    acc = None
    for t in _split(x, parts):
        y = jnp.dot(sel, t, preferred_element_type=F32)
        acc = y if acc is None else acc + y
    return acc


def _dot_sel_rhs(x, sel, parts=3):
    sel = sel.astype(BF16)
    acc = None
    for t in _split(x, parts):
        y = jnp.dot(t, sel, preferred_element_type=F32)
        acc = y if acc is None else acc + y
    return acc


def _dot_sel_nt(x, sel, parts=3):
    sel = sel.astype(BF16)
    acc = None
    for t in _split(x, parts):
        y = lax.dot_general(t, sel, (((1,), (1,)), ((), ())), preferred_element_type=F32)
        acc = y if acc is None else acc + y
    return acc


def _sel_dot_nt(sel, x, parts=3):
    sel = sel.astype(BF16)
    acc = None
    for t in _split(x, parts):
        y = lax.dot_general(sel, t, (((1,), (1,)), ((), ())), preferred_element_type=F32)
        acc = y if acc is None else acc + y
    return acc


def _iota(shape, dim):
    return lax.broadcasted_iota(jnp.int32, shape, dim)


def _div(x, n):
    assert n & (n - 1) == 0
    return x >> (n.bit_length() - 1)


def _params(*semantics):
    return pltpu.CompilerParams(dimension_semantics=semantics, vmem_limit_bytes=VMEM_LIMIT)


def _ada_kernel(c_ref, w_ref, b_ref, o_ref):
    o_ref[...] = _dot(_silu(c_ref[...]), w_ref[...]) + b_ref[...]


def _ada_table(cc, ada_w, ada_b):
    n_l, d, _ = ada_w.shape
    r = cc.shape[0]
    out = pl.pallas_call(
        _ada_kernel,
        out_shape=jax.ShapeDtypeStruct((n_l, N_MOD, r, d), F32),
        grid=(n_l, N_MOD),
        in_specs=[
            pl.BlockSpec((r, d), lambda l, j: (0, 0)),
            pl.BlockSpec((None, d, d), lambda l, j: (l, 0, j)),
            pl.BlockSpec((None, 1, d), lambda l, j: (l, 0, j)),
        ],
        out_specs=pl.BlockSpec((None, None, r, d), lambda l, j: (l, j, 0, 0)),
        compiler_params=_params("parallel", "parallel"),
        name="ada",
    )(cc, ada_w, ada_b.reshape(n_l, 1, N_MOD * d))
    return out.transpose(0, 2, 1, 3)


class _Geom:
    def __init__(self, batch, seq, ctx_len, tm):
        self.batch, self.seq, self.ctx_len, self.tm = batch, seq, ctx_len, tm
        self.n_ctx = batch * ctx_len
        self.n_tok = batch * (ctx_len + seq)
        assert self.n_ctx % tm == 0 and seq % tm == 0
        self.ctx_tiles = self.n_ctx // tm
        self.tiles_per_batch = seq // tm
        self.tiles = self.n_tok // tm

    def mod_row(self, i):
        return jnp.where(i < self.ctx_tiles, self.batch, (i - self.ctx_tiles) // self.tiles_per_batch)

    def rope_block(self, i):
        return jnp.where(i < self.ctx_tiles, self.tiles_per_batch,
                         (i - self.ctx_tiles) % self.tiles_per_batch)


def _ffn_kernel(x_ref, mod_ref, g_ref, wg_ref, wu_ref, wd_ref, o_ref, *, mod_base, g_base):
    x = x_ref[...]
    mod = mod_ref[...]
    g = g_ref[...]
    shift = mod[mod_base:mod_base + 1]
    scale = mod[mod_base + 1:mod_base + 2]
    gate = mod[mod_base + 2:mod_base + 3]
    u = (_rms(x, g[g_base:g_base + 1]) * (1.0 + scale) + shift).astype(BF16)
    d_ff = wg_ref.shape[1]
    y = None
    for c0 in range(0, d_ff, FF_CHUNK):
        a = jnp.dot(u, wg_ref[:, c0:c0 + FF_CHUNK], preferred_element_type=F32)
        b = jnp.dot(u, wu_ref[:, c0:c0 + FF_CHUNK], preferred_element_type=F32)
        hcol = (_silu(a) * b).astype(BF16)
        part = jnp.dot(hcol, wd_ref[c0:c0 + FF_CHUNK, :], preferred_element_type=F32)
        y = part if y is None else y + part
    o_ref[...] = x + FFN_RES * gate * _rms(y, g[g_base + 1:g_base + 2])


def _ffn(tok, modtab, norm_g, wg, wu, wd, geom, layer, mod_base, g_base, latent_only=False):
    tm = geom.tm
    d = tok.shape[1]
    d_ff = wg.shape[2]
    assert d_ff % FF_CHUNK == 0
    first = geom.ctx_tiles if latent_only else 0
    n_tiles = geom.tiles - first
    return pl.pallas_call(
        functools.partial(_ffn_kernel, mod_base=mod_base, g_base=g_base),
        out_shape=jax.ShapeDtypeStruct((n_tiles * tm, d), F32),
        grid=(n_tiles,),
        in_specs=[
            pl.BlockSpec((tm, d), lambda i: (i + first, 0)),
            pl.BlockSpec((None, None, N_MOD, d), lambda i: (layer, geom.mod_row(i + first), 0, 0)),
            pl.BlockSpec((None, 6, d), lambda i: (layer, 0, 0)),
            pl.BlockSpec((None, d, d_ff), lambda i: (layer, 0, 0), pipeline_mode=pl.Buffered(1)),
            pl.BlockSpec((None, d, d_ff), lambda i: (layer, 0, 0), pipeline_mode=pl.Buffered(1)),
            pl.BlockSpec((None, d_ff, d), lambda i: (layer, 0, 0), pipeline_mode=pl.Buffered(1)),
        ],
        out_specs=pl.BlockSpec((tm, d), lambda i: (i, 0)),
        compiler_params=_params("parallel"),
        name="ffn",
    )(tok, modtab, norm_g, wg, wu, wd)


def _inproj_kernel(x_ref, mod_ref, g_ref, w_ref, b_ref, cos_ref, sin_ref,
                   pa_ref, pb_ref, pc_ref, ps_ref):
    x = x_ref[...]
    mod = mod_ref[...]
    g = g_ref[...]
    u = _rms(x, g[2:3]) * (1.0 + mod[4:5]) + mod[3:4]
    p = _dot(u, w_ref[...]) + b_ref[...]
    qk_w = A_OUT + A_KV
    qk = p[:, 0:qk_w]
    reps = qk_w // LANES
    cos = jnp.concatenate([cos_ref[...]] * reps, axis=1)
    sin = jnp.concatenate([sin_ref[...]] * reps, axis=1)
    half = ROPE_AXIS // 2
    first_half = (_iota(qk.shape, 1) & (ROPE_AXIS - 1)) < half
    partner = jnp.where(first_half, pltpu.roll(qk, qk_w - half, 1), pltpu.roll(qk, half, 1))
    rot = qk * cos + partner * sin
    pa_ref[:, 0:A_OUT] = rot[:, 0:A_OUT] * (A_HEAD_DIM ** -0.5)
    pa_ref[:, A_OUT:qk_w] = rot[:, A_OUT:qk_w]
    pa_ref[:, qk_w:PA_W] = p[:, qk_w:PA_W]
    pb_ref[...] = p[:, PA_W:PA_W + PB_W]
    pc_ref[...] = p[:, PA_W + PB_W:PA_W + PB_W + PC_W]
    ps_ref[...] = p[:, PA_W + PB_W + PC_W:]


def _inproj(tok, modtab, norm_g, w_in, b_in, rope_cos, rope_sin, geom, layer):
    tm = geom.tm
    d = tok.shape[1]
    n_tok = tok.shape[0]
    width = w_in.shape[2]
    outs = [jax.ShapeDtypeStruct((n_tok, w), F32) for w in (PA_W, PB_W, PC_W, PS_W)]
    return pl.pallas_call(
        _inproj_kernel,
        out_shape=outs,
        grid=(geom.tiles,),
        in_specs=[
            pl.BlockSpec((tm, d), lambda i: (i, 0)),
            pl.BlockSpec((None, None, N_MOD, d), lambda i: (layer, geom.mod_row(i), 0, 0)),
            pl.BlockSpec((None, 6, d), lambda i: (layer, 0, 0)),
            pl.BlockSpec((None, d, width), lambda i: (layer, 0, 0)),
            pl.BlockSpec((None, 1, width), lambda i: (layer, 0, 0)),
            pl.BlockSpec((tm, LANES), lambda i: (geom.rope_block(i), 0)),
            pl.BlockSpec((tm, LANES), lambda i: (geom.rope_block(i), 0)),
        ],
        out_specs=[pl.BlockSpec((tm, w), lambda i: (i, 0)) for w in (PA_W, PB_W, PC_W, PS_W)],
        compiler_params=_params("parallel"),
        name="inproj",
    )(tok, modtab, norm_g, w_in, b_in, rope_cos, rope_sin)


def _attn_kernel(q_ref, k0_ref, k1_ref, k2_ref, k3_ref, v0_ref, v1_ref, v2_ref, v3_ref, kc_ref, vc_ref,
                 sink_ref, o_ref, *, ctx_steps, seq_len):
    n = pl.program_id(1) - ctx_steps
    qb = ATT_QUERIES
    n_win = 4 * ATT_BLOCK
    hd = A_HEAD_DIM
    k_abs = n * qb - WINDOW + _iota((n_win, qb), 0)
    q_abs = n * qb + _iota((n_win, qb), 1)
    hi = jnp.where(n >= 0, seq_len, 0)
    mask = (jnp.abs(q_abs - k_abs) <= WINDOW) & (k_abs >= 0) & (k_abs < hi)
    q_t = q_ref[...].T
    k_all = jnp.concatenate([k0_ref[...], k1_ref[...], k2_ref[...], k3_ref[...], kc_ref[...]], axis=0)
    v_t = jnp.concatenate([v0_ref[...], v1_ref[...], v2_ref[...], v3_ref[...], vc_ref[...]], axis=0).T
    zero = jnp.zeros((hd, qb), F32)
    cols, sinks = [], []
    for h in range(A_HEADS):
        q_h = q_t[h * hd:(h + 1) * hd]
        cols.append(jnp.concatenate([q_h, zero] if h // A_REP == 0 else [zero, q_h], axis=0))
        sinks.append(jnp.broadcast_to(sink_ref[h:h + 1, 0:1], (1, qb)))
    sk = jnp.concatenate(sinks, axis=1)
    s = _dot(k_all, jnp.concatenate(cols, axis=1))
    s_win = jnp.where(jnp.concatenate([mask] * A_HEADS, axis=1), s[:n_win], -jnp.inf)
    s_ctx = s[n_win:]
    m = jnp.maximum(jnp.maximum(jnp.max(s_win, axis=0, keepdims=True),
                                jnp.max(s_ctx, axis=0, keepdims=True)), sk)
    e_win = jnp.exp(s_win - m)
    e_ctx = jnp.exp(s_ctx - m)
    den = (jnp.sum(e_win, axis=0, keepdims=True) + jnp.sum(e_ctx, axis=0, keepdims=True)
           + jnp.exp(sk - m))
    p = jnp.concatenate([e_win, e_ctx], axis=0).astype(BF16)
    per_group = A_REP * qb
    outs = []
    for g in range(A_KV_HEADS):
        lanes = slice(g * per_group, (g + 1) * per_group)
        o_g = _dot(v_t[g * hd:(g + 1) * hd], p[:, lanes]) / den[:, lanes]
        outs += [o_g[:, r * qb:(r + 1) * qb] for r in range(A_REP)]
    o_ref[...] = jnp.concatenate(outs, axis=0).T


def _attn(pa, sink_rows, batch, seq, ctx_len):
    n_tok = pa.shape[0]
    qb = ATT_QUERIES
    blk = ATT_BLOCK
    per_q = qb // blk
    ctx_steps = ctx_len // qb
    seq_steps = seq // qb
    seq_blocks = seq // blk
    lat0 = batch * (ctx_len // blk)
    k_col = A_OUT // LANES
    v_col = k_col + 1

    def q_map(b, j):
        return (jnp.where(j < ctx_steps, b * ctx_steps + j, batch * ctx_steps + b * seq_steps + j - ctx_steps), 0)

    def kv_map(i, col):
        def f(b, j):
            n = jnp.clip((j - ctx_steps) * per_q - 1 + i, 0, seq_blocks - 1)
            return (lat0 + b * seq_blocks + n, col)
        return f

    kv_specs = [pl.BlockSpec((blk, LANES), kv_map(i, col))
                for col in (k_col, v_col) for i in range(per_q + 2)]
    return pl.pallas_call(
        functools.partial(_attn_kernel, ctx_steps=ctx_steps, seq_len=seq),
        out_shape=jax.ShapeDtypeStruct((n_tok, A_OUT), F32),
        grid=(batch, ctx_steps + seq_steps),
        in_specs=[pl.BlockSpec((qb, A_OUT), q_map)] + kv_specs + [
            pl.BlockSpec((ctx_len, LANES), lambda b, j: (b, k_col)),
            pl.BlockSpec((ctx_len, LANES), lambda b, j: (b, v_col)),
            pl.BlockSpec((SUBLANES, LANES), lambda b, j: (0, 0)),
        ],
        out_specs=pl.BlockSpec((qb, A_OUT), q_map),
        compiler_params=_params("parallel", "parallel"),
        name="attn",
    )(*([pa] * (2 * (per_q + 2) + 3)), sink_rows)


def _seq_block(b, j, reverse, batch, ctx_blocks, seq_blocks):
    jc = jnp.where(reverse, ctx_blocks - 1 - j, j)
    jl = jnp.where(reverse, seq_blocks - 1 - (j - ctx_blocks), j - ctx_blocks)
    return jnp.where(j < ctx_blocks, b * ctx_blocks + jc, batch * ctx_blocks + b * seq_blocks + jl)


def _gla_kernel(pbf_ref, psf_ref, pbb_ref, psb_ref, wa_ref, ba_ref, of_ref, ob_ref, st_ref):
    @pl.when(pl.program_id(1) == 0)
    def _():
        st_ref[...] = jnp.zeros_like(st_ref)

    _round_robin(_gla_chain(pbf_ref, psf_ref, wa_ref.at[0], ba_ref.at[0], of_ref, st_ref.at[0], reverse=False),
                 _gla_chain(pbb_ref, psb_ref, wa_ref.at[1], ba_ref.at[1], ob_ref, st_ref.at[1], reverse=True))


def _gla_chain(pb_ref, ps_ref, wa_ref, ba_ref, o_ref, st_ref, *, reverse):
    blk = SEQ_BLOCK
    ch = GLA_CHUNK
    half = ch // 2
    assert half == SUBLANES
    n_ch = blk // ch
    shift = ch.bit_length() - 1
    q = pb_ref[:, 0:B_KW] * (B_DK ** -0.5)
    k = pb_ref[:, B_KW:2 * B_KW]
    v = pb_ref[:, 2 * B_KW:2 * B_KW + B_VW]
    log_a = _log_sigmoid(_dot(ps_ref[...], wa_ref[...]) + ba_ref[...]) * (1.0 / GLA_TAU)

    r = _iota((blk, blk), 0)
    c = _iota((blk, blk), 1)
    same = (r >> shift) == (c >> shift)
    tri = same & ((c >= r) if reverse else (c <= r))
    bcum = _dot_sel(tri.astype(F32), log_a)
    btot = _dot_sel(same.astype(F32), log_a)
    q_in = q * jnp.exp(bcum)
    k_out = k * jnp.exp(btot - bcum)
    v_t = v.T

    row_chunk = _iota((blk, B_KW), 0) >> shift
    t_full = _iota((ch, B_KW), 0)
    t_half = _iota((half, B_KW), 0)
    head_expand = _div(_iota((B_KW, B_VW), 0), B_DK) == _div(_iota((B_KW, B_VW), 1), B_DV)
    head_expand = head_expand.astype(BF16)
    state_mask = _div(_iota((B_VW, B_KW), 0), B_DV) == _div(_iota((B_VW, B_KW), 1), B_DK)
    full_src = range(half, ch) if reverse else range(half)
    half_src = range(half) if reverse else range(half, ch)
    h0 = 0 if reverse else half
    yield

    st = st_ref[...]
    outs = [None] * n_ch
    for ci in (range(n_ch - 1, -1, -1) if reverse else range(n_ch)):
        lo = ci * ch
        q_c, k_c, b_c, v_c = q[lo:lo + ch], k[lo:lo + ch], bcum[lo:lo + ch], v[lo:lo + ch]
        q_h, b_h = q_c[h0:h0 + half], b_c[h0:h0 + half]
        pieces = []
        for s in full_src:
            seen = (t_full <= s) if reverse else (t_full >= s)
            arg = jnp.where(seen, b_c - b_c[s:s + 1], -jnp.inf)
            pieces.append(q_c * k_c[s:s + 1] * jnp.exp(arg))
        for s in half_src:
            seen = (t_half <= s - h0) if reverse else (t_half >= s - h0)
            arg = jnp.where(seen, b_h - b_c[s:s + 1], -jnp.inf)
            pieces.append(q_h * k_c[s:s + 1] * jnp.exp(arg))
        w = _dot(jnp.concatenate(pieces, axis=0), head_expand)
        o_c = _dot_nt(q_in[lo:lo + ch], st)
        o_halves = [o_c[0:half], o_c[half:ch]]
        row = 0
        for s in full_src:
            for part in range(2):
                o_halves[part] = o_halves[part] + w[row:row + half] * v_c[s:s + 1]
                row += half
        for s in half_src:
            part = h0 // half
            o_halves[part] = o_halves[part] + w[row:row + half] * v_c[s:s + 1]
            row += half
        outs[ci] = jnp.concatenate(o_halves, axis=0)
        st_loc = _dot(v_t, jnp.where(row_chunk == ci, k_out, 0.0))
        st = st * jnp.exp(btot[lo:lo + 1]) + jnp.where(state_mask, st_loc, 0.0)
        yield
    st_ref[...] = st
    o_ref[...] = jnp.concatenate(outs, axis=0)


def _gla(pb, ps, wa_pad, ba, batch, seq, ctx_len, layer):
    n_tok = pb.shape[0]
    blk = SEQ_BLOCK
    ctx_blocks, seq_blocks = ctx_len // blk, seq // blk

    def chain_specs(reverse):
        def blk_map(b, j):
            return (_seq_block(b, j, reverse, batch, ctx_blocks, seq_blocks), 0)
        return [pl.BlockSpec((blk, 2 * B_KW + B_VW), blk_map), pl.BlockSpec((blk, PS_W), blk_map)], \
            pl.BlockSpec((blk, B_VW), blk_map)

    in_f, out_f = chain_specs(False)
    in_b, out_b = chain_specs(True)
    out = jax.ShapeDtypeStruct((n_tok, B_VW), F32)
    return pl.pallas_call(
        _gla_kernel,
        out_shape=[out, out],
        grid=(batch, ctx_blocks + seq_blocks),
        in_specs=in_f + in_b + [
            pl.BlockSpec((None, 2, PS_W, B_KW), lambda b, j: (layer, 0, 0, 0)),
            pl.BlockSpec((None, 2, 1, B_KW), lambda b, j: (layer, 0, 0, 0)),
        ],
        out_specs=[out_f, out_b],
        scratch_shapes=[pltpu.VMEM((2, B_VW, B_KW), F32)],
        compiler_params=_params("parallel", "arbitrary"),
        name="gla",
    )(pb, ps, pb, ps, wa_pad, ba)


def _lane_scan(x, op, identity, reverse):
    n = x.shape[1]
    lane = _iota(x.shape, 1)
    k = 1
    while k < n:
        shifted = pltpu.roll(x, n - k if reverse else k, 1)
        inside = (lane < n - k) if reverse else (lane >= k)
        x = op(x, jnp.where(inside, shifted, identity))
        k *= 2
        yield
    return x


def _round_robin(*chains):
    live = list(range(len(chains)))
    results = [None] * len(chains)
    while live:
        for i in list(live):
            try:
                next(chains[i])
            except StopIteration as stop:
                results[i] = stop.value
                live.remove(i)
    return results


def _mlstm_kernel(pcf_ref, prevf_ref, nextf_ref, psf_ref, pcb_ref, prevb_ref, nextb_ref, psb_ref,
                  cw_ref, sel_ref, fb_ref, of_ref, ob_ref, c_ref, n_ref, m_ref, rows_ref, stat_ref,
                  *, ctx_blocks, seq_blocks):
    j = pl.program_id(1)

    @pl.when(j == 0)
    def _():
        c_ref[...] = jnp.zeros_like(c_ref)
        n_ref[...] = jnp.zeros_like(n_ref)
        m_ref[...] = jnp.zeros_like(m_ref)

    in_ctx = j < ctx_blocks
    n_seq = jnp.where(in_ctx, ctx_blocks, seq_blocks)
    jj = jnp.where(in_ctx, j, j - ctx_blocks)
    _, _, qkv_f, qkv_b = _round_robin(
        _mlstm_gates(psf_ref, sel_ref.at[0], fb_ref.at[0], rows_ref.at[0], stat_ref.at[0], rev=False),
        _mlstm_gates(psb_ref, sel_ref.at[1], fb_ref.at[1], rows_ref.at[1], stat_ref.at[1], rev=True),
        _mlstm_conv(pcf_ref, prevf_ref, nextf_ref, cw_ref, pos=jj, n_seq=n_seq),
        _mlstm_conv(pcb_ref, prevb_ref, nextb_ref, cw_ref, pos=n_seq - 1 - jj, n_seq=n_seq))
    _round_robin(
        _mlstm_step(qkv_f, rows_ref.at[0], stat_ref.at[0], of_ref, c_ref.at[0], n_ref.at[0], m_ref.at[0],
                    rev=False),
        _mlstm_step(qkv_b, rows_ref.at[1], stat_ref.at[1], ob_ref, c_ref.at[1], n_ref.at[1], m_ref.at[1],
                    rev=True))


def _mlstm_conv(pc_ref, prev_ref, next_ref, cw_ref, *, pos, n_seq):
    blk = SEQ_BLOCK
    half = blk // 2
    prev_row = jnp.where(pos > 0, prev_ref[SUBLANES - 1:SUBLANES, :], 0.0)
    next_row = jnp.where(pos < n_seq - 1, next_ref[0:1, :], 0.0)
    row = _iota((half, LANES), 0)
    tiles = []
    for c0 in range(0, 2 * C_W, LANES):
        lanes = slice(c0, c0 + LANES)
        halves = []
        for r0 in (0, half):
            x = pc_ref[r0:r0 + half, lanes]
            above = prev_row[:, lanes] if r0 == 0 else pc_ref[r0 - 1:r0, lanes]
            below = next_row[:, lanes] if r0 + half == blk else pc_ref[r0 + half:r0 + half + 1, lanes]
            x_prev = jnp.where(row == 0, above, pltpu.roll(x, 1, 0))
            x_next = jnp.where(row == half - 1, below, pltpu.roll(x, half - 1, 0))
            halves.append(_silu(x_prev * cw_ref[0:1, lanes] + x * cw_ref[1:2, lanes]
                                + x_next * cw_ref[2:3, lanes]))
            yield
        tiles.append(jnp.concatenate(halves, axis=0))
    q = jnp.concatenate(tiles[:C_W // LANES], axis=1)
    k = jnp.concatenate(tiles[C_W // LANES:], axis=1) * (C_DH ** -0.5)
    v = pc_ref[:, 2 * C_W:3 * C_W]
    head_of_lane = _div(_iota((blk, C_W), 1), C_DH)
    qk = []
    for h in range(C_HEADS):
        qk.append(_dot_nt(q, jnp.where(head_of_lane == h, k, 0.0)))
        yield
    return q, k, v, qk


def _mlstm_gates(ps_ref, sel_ref, fb_ref, rows_ref, stat_ref, *, rev):
    sub, blk, nh = SUBLANES, SEQ_BLOCK, C_HEADS
    g = _sel_dot_nt(sel_ref[...], ps_ref[...])
    li = g[0:sub]
    lf = jnp.where(_iota((sub, blk), 0) < nh, _log_sigmoid(g[sub:2 * sub] + fb_ref[:, 0:1]), 0.0)
    yield
    src_tok = _iota((blk, blk), 0)
    dst_tok = _iota((blk, blk), 1)
    upto = (src_tok >= dst_tok) if rev else (src_tok <= dst_tok)
    bcum = _dot_sel_rhs(lf, upto.astype(F32))
    yield
    btot = jnp.sum(lf, axis=1, keepdims=True)
    u = li - bcum
    u_max = yield from _lane_scan(u, jnp.maximum, -jnp.inf, rev)
    g_end = btot - bcum + li
    m_loc = jnp.max(g_end, axis=1, keepdims=True)
    rows_ref[...] = jnp.concatenate([u, u_max, bcum, jnp.exp(g_end - m_loc)], axis=0)
    stat_ref[...] = jnp.concatenate([jnp.broadcast_to(btot, (sub, LANES)),
                                     jnp.broadcast_to(m_loc, (sub, LANES))], axis=0)


def _mlstm_step(qkv, rows_ref, stat_ref, o_ref, c_ref, n_ref, m_ref, *, rev):
    blk = SEQ_BLOCK
    nh, dh = C_HEADS, C_DH
    sub = SUBLANES
    q, k, v, qk = qkv
    u = rows_ref[0:sub, :]
    u_max = rows_ref[sub:2 * sub, :]
    bcum = rows_ref[2 * sub:3 * sub, :]
    w_end = rows_ref[3 * sub:4 * sub, :]
    btot = stat_ref[0:sub, 0:1]
    m_loc = stat_ref[sub:2 * sub, 0:1]
    yield

    m_prev = m_ref[:, 0:1]
    big_m = jnp.maximum(u_max, m_prev)
    a_t = jnp.exp(m_prev - big_m)
    neg_m_t = -(bcum + big_m)
    m_new = jnp.maximum(btot + m_prev, m_loc)
    a_st = jnp.exp(btot + m_prev - m_new)
    e_st = jnp.exp(m_loc - m_new)

    pad = jnp.zeros((LANES - 4 * sub, blk), F32)
    yield
    cols = jnp.concatenate([big_m, a_t, neg_m_t, w_end, pad], axis=0).T
    yield
    width = nh * blk + 3 * C_W
    src_row = _iota((LANES, width), 0)
    lane = _iota((LANES, width), 1)
    tail = lane - nh * blk
    src = jnp.where(tail < 0, _div(lane, blk), sub * (1 + _div(tail, C_W)) + _div(tail & (C_W - 1), dh))
    wide = _dot_sel_rhs(cols, (src_row == src).astype(F32))
    a_all = wide[:, nh * blk:nh * blk + C_W]
    floor_all = jnp.exp(wide[:, nh * blk + C_W:nh * blk + 2 * C_W])
    w_all = wide[:, nh * blk + 2 * C_W:]

    c_t = c_ref[...]
    n_row = n_ref[0:1, :]
    rr = _iota((blk, blk), 0)
    cc = _iota((blk, blk), 1)
    causal = (rr <= cc) if rev else (cc <= rr)
    head_of_lane = _div(_iota((blk, C_W), 1), dh)
    num = a_all * _dot(q, c_t)
    row_sum = None
    for h in range(nh):
        in_head = head_of_lane == h
        e = jnp.where(causal, u[h:h + 1, :] - wide[:, h * blk:(h + 1) * blk], -jnp.inf)
        sc = (qk[h] * jnp.exp(e)).astype(BF16)
        num = num + jnp.dot(sc, jnp.where(in_head, v, 0.0).astype(BF16), preferred_element_type=F32)
        rs = jnp.dot(sc, in_head.astype(BF16), preferred_element_type=F32)
        row_sum = rs if row_sum is None else row_sum + rs
        yield
    same_head = _div(_iota((C_W, C_W), 0), dh) == _div(_iota((C_W, C_W), 1), dh)
    den = a_all * _dot(q * n_row, same_head) + row_sum
    o_ref[...] = num / jnp.maximum(jnp.abs(den), floor_all)
    yield

    head8 = _div(_iota((sub, C_W), 1), dh) == _iota((sub, C_W), 0)
    a_lane = jnp.sum(jnp.where(head8, a_st, 0.0), axis=0, keepdims=True)
    e_lane = jnp.sum(jnp.where(head8, e_st, 0.0), axis=0, keepdims=True)
    c_loc = _dot(k.T, v * w_all)
    c_ref[...] = a_lane * c_t + jnp.where(same_head, e_lane * c_loc, 0.0)
    n_ref[0:1, :] = a_lane * n_row + e_lane * jnp.sum(k * w_all, axis=0, keepdims=True)
    m_ref[...] = jnp.broadcast_to(m_new, m_ref.shape)


def _mlstm(pc, ps, conv_w, gate_sel, f_bias_rows, batch, seq, ctx_len, layer):
    n_tok = pc.shape[0]
    blk = SEQ_BLOCK
    ctx_blocks, seq_blocks = ctx_len // blk, seq // blk
    rows8 = blk // SUBLANES
    last8 = n_tok // SUBLANES - 1

    def chain_specs(reverse):
        def blk_of(b, j):
            return _seq_block(b, j, reverse, batch, ctx_blocks, seq_blocks)
        return [
            pl.BlockSpec((blk, 3 * C_W), lambda b, j: (blk_of(b, j), 0)),
            pl.BlockSpec((SUBLANES, 2 * C_W), lambda b, j: (jnp.maximum(blk_of(b, j) * rows8 - 1, 0), 0)),
            pl.BlockSpec((SUBLANES, 2 * C_W), lambda b, j: (jnp.minimum((blk_of(b, j) + 1) * rows8, last8), 0)),
            pl.BlockSpec((blk, PS_W), lambda b, j: (blk_of(b, j), 0)),
        ], pl.BlockSpec((blk, C_W), lambda b, j: (blk_of(b, j), 0))

    in_f, out_f = chain_specs(False)
    in_b, out_b = chain_specs(True)
    out = jax.ShapeDtypeStruct((n_tok, C_W), F32)
    return pl.pallas_call(
        functools.partial(_mlstm_kernel, ctx_blocks=ctx_blocks, seq_blocks=seq_blocks),
        out_shape=[out, out],
        grid=(batch, ctx_blocks + seq_blocks),
        in_specs=in_f + in_b + [
            pl.BlockSpec((None, 3, 2 * C_W), lambda b, j: (layer, 0, 0)),
            pl.BlockSpec((2, 2 * SUBLANES, PS_W), lambda b, j: (0, 0, 0)),
            pl.BlockSpec((None, 2, SUBLANES, LANES), lambda b, j: (layer, 0, 0, 0)),
        ],
        out_specs=[out_f, out_b],
        scratch_shapes=[pltpu.VMEM((2, C_W, C_W), F32), pltpu.VMEM((2, SUBLANES, C_W), F32),
                        pltpu.VMEM((2, SUBLANES, LANES), F32), pltpu.VMEM((2, 4 * SUBLANES, blk), F32),
                        pltpu.VMEM((2, 2 * SUBLANES, LANES), F32)],
        compiler_params=_params("parallel", "arbitrary"),
        name="mlstm",
    )(pc, pc, pc, ps, pc, pc, pc, ps, conv_w, gate_sel, f_bias_rows)


def _head_layernorm(o, g, width):
    n = o.shape[1]
    avg = jnp.where(_div(_iota((n, n), 0), width) == _div(_iota((n, n), 1), width), 1.0 / width, 0.0)
    mu = _dot_sel_rhs(o, avg, parts=2)
    cen = o - mu
    var = _dot_sel_rhs(cen * cen, avg, parts=2)
    return cen * lax.rsqrt(var + EPS) * g


def _out_kernel(x_ref, mod_ref, g_ref, ya_ref, gb_ref, obf_ref, obb_ref, gc_ref, ocf_ref, ocb_ref,
                gng_ref, mng_ref, w_ref, o_ref):
    y_b = _silu(gb_ref[...]) * _head_layernorm(obf_ref[...] + obb_ref[...], gng_ref[...], B_DV)
    y_c = _sigmoid(gc_ref[...]) * _head_layernorm(ocf_ref[...] + ocb_ref[...], mng_ref[...], C_DH)
    y = (_dot(ya_ref[...], w_ref[0:A_OUT, :])
         + _dot(y_b, w_ref[A_OUT:A_OUT + B_VW, :])
         + _dot(y_c, w_ref[A_OUT + B_VW:, :]))
    o_ref[...] = x_ref[...] + mod_ref[5:6, :] * _rms(y, g_ref[3:4, :])


def _out(tok, modtab, norm_g, ya, pb, ob_f, ob_b, pc, oc_f, oc_b, gla_norm, mlstm_norm, w_out, geom, layer):
    tm = geom.tm
    n_tok, d = tok.shape
    gb_col = (2 * B_KW + B_VW) // B_VW
    gc_col = 3
    return pl.pallas_call(
        _out_kernel,
        out_shape=jax.ShapeDtypeStruct((n_tok, d), F32),
        grid=(geom.tiles,),
        in_specs=[
            pl.BlockSpec((tm, d), lambda i: (i, 0)),
            pl.BlockSpec((None, None, N_MOD, d), lambda i: (layer, geom.mod_row(i), 0, 0)),
            pl.BlockSpec((None, 6, d), lambda i: (layer, 0, 0)),
            pl.BlockSpec((tm, A_OUT), lambda i: (i, 0)),
            pl.BlockSpec((tm, B_VW), lambda i: (i, gb_col)),
            pl.BlockSpec((tm, B_VW), lambda i: (i, 0)),
            pl.BlockSpec((tm, B_VW), lambda i: (i, 0)),
            pl.BlockSpec((tm, C_W), lambda i: (i, gc_col)),
            pl.BlockSpec((tm, C_W), lambda i: (i, 0)),
            pl.BlockSpec((tm, C_W), lambda i: (i, 0)),
            pl.BlockSpec((None, 1, B_VW), lambda i: (layer, 0, 0)),
            pl.BlockSpec((None, 1, C_W), lambda i: (layer, 0, 0)),
            pl.BlockSpec((None, d, d), lambda i: (layer, 0, 0)),
        ],
        out_specs=pl.BlockSpec((tm, d), lambda i: (i, 0)),
        compiler_params=_params("parallel"),
        name="out",
    )(tok, modtab, norm_g, ya, pb, ob_f, ob_b, pc, oc_f, oc_b, gla_norm, mlstm_norm, w_out)


def _rope_tables(seq, tm):
    pos = jnp.arange(seq)
    half = ROPE_AXIS // 2
    inv_freq = ROPE_BASE ** (-jnp.arange(half, dtype=F32) / half)
    ang_r = (pos // GRID_W).astype(F32)[:, None] * inv_freq[None, :]
    ang_c = (pos % GRID_W).astype(F32)[:, None] * inv_freq[None, :]
    cos = jnp.concatenate([jnp.cos(ang_r)] * 2 + [jnp.cos(ang_c)] * 2, axis=1)
    sin = jnp.concatenate([-jnp.sin(ang_r), jnp.sin(ang_r), -jnp.sin(ang_c), jnp.sin(ang_c)], axis=1)
    reps = LANES // A_HEAD_DIM
    cos = jnp.concatenate([jnp.tile(cos, (1, reps)), jnp.ones((tm, LANES), F32)], axis=0)
    sin = jnp.concatenate([jnp.tile(sin, (1, reps)), jnp.zeros((tm, LANES), F32)], axis=0)
    return cos, sin


def _regroup_in(w):
    a_end = PA_W + PB_W
    alpha = slice(a_end, a_end + 2 * GLA_RANK)
    c_qkv = slice(alpha.stop, alpha.stop + 3 * C_W)
    c_gate = slice(c_qkv.stop, c_qkv.stop + 4 * C_HEADS)
    c_og = slice(c_gate.stop, c_gate.stop + C_W)
    pad = jnp.zeros(w.shape[:-1] + (PS_W - 2 * GLA_RANK - 4 * C_HEADS,), w.dtype)
    return jnp.concatenate([w[..., :a_end], w[..., c_qkv], w[..., c_og], w[..., alpha], w[..., c_gate], pad],
                           axis=-1)


def _pick_tile(n_ctx, seq, largest):
    tm = largest
    while tm >= LANES:
        if n_ctx % tm == 0 and seq % tm == 0:
            return tm
        tm //= 2
    raise ValueError("token counts must be multiples of 128")


def kernel(x, c, ctx, c_ctx, ada_w, ada_b, norm_g, ffn1_wg, ffn1_wu, ffn1_wd, ffn2_wg, ffn2_wu, ffn2_wd,
           w_in, b_in, w_out, attn_sink, gla_w_alpha, gla_b_alpha, gla_norm, mlstm_conv, mlstm_f_bias,
           mlstm_norm):
    batch, seq, d = x.shape
    ctx_len = ctx.shape[1]
    n_l = ada_w.shape[0]
    assert seq % SEQ_BLOCK == 0 and ctx_len % SEQ_BLOCK == 0
    geom = _Geom(batch, seq, ctx_len, _pick_tile(batch * ctx_len, seq, TOKEN_TILE))
    geom_ffn = _Geom(batch, seq, ctx_len, _pick_tile(batch * ctx_len, seq, FFN_TOKEN_TILE))

    rows = -(-(batch + 1) // SUBLANES) * SUBLANES
    cc = jnp.concatenate([c, c_ctx[None, :], jnp.zeros((rows - batch - 1, d), F32)], axis=0)
    modtab = _ada_table(cc, ada_w, ada_b)

    rope_cos, rope_sin = _rope_tables(seq, geom.tm)
    w_in_r = _regroup_in(w_in.astype(BF16))
    b_in_r = _regroup_in(b_in)[:, None, :]
    wg1, wu1, wd1 = ffn1_wg.astype(BF16), ffn1_wu.astype(BF16), ffn1_wd.astype(BF16)
    wg2, wu2, wd2 = ffn2_wg.astype(BF16), ffn2_wu.astype(BF16), ffn2_wd.astype(BF16)
    w_out_b = w_out.astype(BF16)

    wa_pad = jnp.zeros((n_l, 2, PS_W, B_KW), F32)
    for dirn in range(2):
        wa_pad = wa_pad.at[:, dirn, dirn * GLA_RANK:(dirn + 1) * GLA_RANK, :].set(gla_w_alpha[:, dirn])
    wa_pad = wa_pad.astype(BF16)
    ba = gla_b_alpha[:, :, None, :]

    nh = C_HEADS
    out_idx = jnp.arange(2 * SUBLANES)[:, None]
    src_idx = jnp.arange(PS_W)[None, :]
    head_idx = out_idx % SUBLANES
    gate_sel = jnp.stack([
        (head_idx < nh) & (src_idx == PS_GATE0 + 2 * nh * dirn + nh * (out_idx // SUBLANES) + head_idx)
        for dirn in range(2)]).astype(F32)
    f_bias_rows = jnp.zeros((n_l, 2, SUBLANES), F32).at[:, :, 0:nh].set(mlstm_f_bias)
    f_bias_rows = jnp.broadcast_to(f_bias_rows[:, :, :, None], (n_l, 2, SUBLANES, LANES))

    sink_rows = jnp.broadcast_to(attn_sink[:, :, None], (n_l, A_HEADS, LANES))
    gla_g = gla_norm[:, None, :]
    mlstm_g = mlstm_norm[:, None, :]

    tok = jnp.concatenate([ctx.reshape(batch * ctx_len, d), x.reshape(batch * seq, d)], axis=0)
    for layer in range(n_l):
        last = layer == n_l - 1
        tok = _ffn(tok, modtab, norm_g, wg1, wu1, wd1, geom_ffn, layer, 0, 0)
        pa, pb, pc, ps = _inproj(tok, modtab, norm_g, w_in_r, b_in_r, rope_cos, rope_sin, geom, layer)
        ya = _attn(pa, sink_rows[layer], batch, seq, ctx_len)
        ob_f, ob_b = _gla(pb, ps, wa_pad, ba, batch, seq, ctx_len, layer)
        oc_f, oc_b = _mlstm(pc, ps, mlstm_conv, gate_sel, f_bias_rows, batch, seq, ctx_len, layer)
        tok = _out(tok, modtab, norm_g, ya, pb, ob_f, ob_b, pc, oc_f, oc_b, gla_g, mlstm_g, w_out_b, geom, layer)
        tok = _ffn(tok, modtab, norm_g, wg2, wu2, wd2, geom_ffn, layer, 6, 4, latent_only=last)
    return tok.reshape(batch, seq, d)
```

```python
import functools

import jax
import jax.numpy as jnp
from jax import lax
from jax.experimental import pallas as pl
from jax.experimental.pallas import tpu as pltpu

F32 = jnp.float32
BF16 = jnp.bfloat16

GRID_W = 64
A_HEAD_DIM = 64
A_HEADS = 8
A_KV_HEADS = 2
A_REP = A_HEADS // A_KV_HEADS
WINDOW = 128
ROPE_BASE = 10000.0
ROPE_AXIS = A_HEAD_DIM // 2
B_DK = 32
B_DV = 64
B_HEADS = 4
GLA_RANK = 16
GLA_TAU = 16.0
GLA_CHUNK = 16
C_DH = 64
C_HEADS = 4
A_OUT = A_HEADS * A_HEAD_DIM
A_KV = A_KV_HEADS * A_HEAD_DIM
B_KW = B_HEADS * B_DK
B_VW = B_HEADS * B_DV
C_W = C_HEADS * C_DH
FFN_RES = 0.5
N_MOD = 9
EPS = 1e-6
LOG2_E = 1.4426950408889634

PA_W = A_OUT + 2 * A_KV
PB_W = 2 * B_KW + 2 * B_VW
PC_W = 4 * C_W
PS_W = 128
PS_GATE0 = 2 * GLA_RANK

LANES = 128
SUBLANES = 8
SEQ_BLOCK = 256
ATT_BLOCK = WINDOW
ATT_QUERIES = 2 * ATT_BLOCK
FF_CHUNK = 256
TOKEN_TILE = 512
FFN_TOKEN_TILE = 1024
V7X_VMEM_BYTES = 64 * 1024 * 1024
VMEM_LIMIT = V7X_VMEM_BYTES * 7 // 8


def _sigmoid(x):
    return 1.0 / (1.0 + jnp.exp(-x))


def _silu(x):
    return x * _sigmoid(x)


def _log_sigmoid(x):
    return jnp.minimum(x, 0.0) - jnp.log(1.0 + jnp.exp(-jnp.abs(x)))


def _rms(x, g):
    return x * lax.rsqrt(jnp.mean(x * x, axis=-1, keepdims=True) + EPS) * g


def _dot(a, b):
    return jnp.dot(a.astype(BF16), b.astype(BF16), preferred_element_type=F32)


def _dot_nt(a, b):
    return lax.dot_general(a.astype(BF16), b.astype(BF16), (((1,), (1,)), ((), ())),
                           preferred_element_type=F32)


def _split(x, parts):
    out = []
    for _ in range(parts):
        t = x.astype(BF16)
        out.append(t)
        x = x - t.astype(F32)
    return out


def _dot_sel(sel, x, parts=3):
    sel = sel.astype(BF16)
    acc = None
    for t in _split(x, parts):
        y = jnp.dot(sel, t, preferred_element_type=F32)
        acc = y if acc is None else acc + y
    return acc


def _dot_sel_rhs(x, sel, parts=3):
    sel = sel.astype(BF16)
    acc = None
    for t in _split(x, parts):
        y = jnp.dot(t, sel, preferred_element_type=F32)
        acc = y if acc is None else acc + y
    return acc


def _dot_sel_nt(x, sel, parts=3):
    sel = sel.astype(BF16)
    acc = None
    for t in _split(x, parts):
        y = lax.dot_general(t, sel, (((1,), (1,)), ((), ())), preferred_element_type=F32)
        acc = y if acc is None else acc + y
    return acc


def _sel_dot_nt(sel, x, parts=3):
    sel = sel.astype(BF16)
    acc = None
    for t in _split(x, parts):
        y = lax.dot_general(sel, t, (((1,), (1,)), ((), ())), preferred_element_type=F32)
        acc = y if acc is None else acc + y
    return acc


def _iota(shape, dim):
    return lax.broadcasted_iota(jnp.int32, shape, dim)


def _div(x, n):
    assert n & (n - 1) == 0
    return x >> (n.bit_length() - 1)


def _params(*semantics):
    return pltpu.CompilerParams(dimension_semantics=semantics, vmem_limit_bytes=VMEM_LIMIT)


def _ada_kernel(c_ref, w_ref, b_ref, o_ref):
    o_ref[...] = _dot(_silu(c_ref[...]), w_ref[...]) + b_ref[...]


def _ada_table(cc, ada_w, ada_b):
    n_l, d, _ = ada_w.shape
    r = cc.shape[0]
    out = pl.pallas_call(
        _ada_kernel,
        out_shape=jax.ShapeDtypeStruct((n_l, N_MOD, r, d), F32),
        grid=(n_l, N_MOD),
        in_specs=[
            pl.BlockSpec((r, d), lambda l, j: (0, 0)),
            pl.BlockSpec((None, d, d), lambda l, j: (l, 0, j)),
            pl.BlockSpec((None, 1, d), lambda l, j: (l, 0, j)),
        ],
        out_specs=pl.BlockSpec((None, None, r, d), lambda l, j: (l, j, 0, 0)),
        compiler_params=_params("parallel", "parallel"),
        name="ada",
    )(cc, ada_w, ada_b.reshape(n_l, 1, N_MOD * d))
    return out.transpose(0, 2, 1, 3)


class _Geom:
    def __init__(self, batch, seq, ctx_len, tm):
        self.batch, self.seq, self.ctx_len, self.tm = batch, seq, ctx_len, tm
        self.n_ctx = batch * ctx_len
        self.n_tok = batch * (ctx_len + seq)
        assert self.n_ctx % tm == 0 and seq % tm == 0
        self.ctx_tiles = self.n_ctx // tm
        self.tiles_per_batch = seq // tm
        self.tiles = self.n_tok // tm

    def mod_row(self, i):
        return jnp.where(i < self.ctx_tiles, self.batch, (i - self.ctx_tiles) // self.tiles_per_batch)

    def rope_block(self, i):
        return jnp.where(i < self.ctx_tiles, self.tiles_per_batch,
                         (i - self.ctx_tiles) % self.tiles_per_batch)


def _ffn_kernel(x_ref, mod_ref, g_ref, wg_ref, wu_ref, wd_ref, o_ref, *, mod_base, g_base):
    x = x_ref[...]
    mod = mod_ref[...]
    g = g_ref[...]
    shift = mod[mod_base:mod_base + 1]
    scale = mod[mod_base + 1:mod_base + 2]
    gate = mod[mod_base + 2:mod_base + 3]
    u = (_rms(x, g[g_base:g_base + 1]) * (1.0 + scale) + shift).astype(BF16)
    d_ff = wg_ref.shape[1]
    y = None
    for c0 in range(0, d_ff, FF_CHUNK):
        a = jnp.dot(u, wg_ref[:, c0:c0 + FF_CHUNK], preferred_element_type=F32)
        b = jnp.dot(u, wu_ref[:, c0:c0 + FF_CHUNK], preferred_element_type=F32)
        hcol = (_silu(a) * b).astype(BF16)
        part = jnp.dot(hcol, wd_ref[c0:c0 + FF_CHUNK, :], preferred_element_type=F32)
        y = part if y is None else y + part
    o_ref[...] = x + FFN_RES * gate * _rms(y, g[g_base + 1:g_base + 2])


def _ffn(tok, modtab, norm_g, wg, wu, wd, geom, layer, mod_base, g_base, latent_only=False):
    tm = geom.tm
    d = tok.shape[1]
    d_ff = wg.shape[2]
    assert d_ff % FF_CHUNK == 0
    first = geom.ctx_tiles if latent_only else 0
    n_tiles = geom.tiles - first
    return pl.pallas_call(
        functools.partial(_ffn_kernel, mod_base=mod_base, g_base=g_base),
        out_shape=jax.ShapeDtypeStruct((n_tiles * tm, d), F32),
        grid=(n_tiles,),
        in_specs=[
            pl.BlockSpec((tm, d), lambda i: (i + first, 0)),
            pl.BlockSpec((None, None, N_MOD, d), lambda i: (layer, geom.mod_row(i + first), 0, 0)),
            pl.BlockSpec((None, 6, d), lambda i: (layer, 0, 0)),
            pl.BlockSpec((None, d, d_ff), lambda i: (layer, 0, 0), pipeline_mode=pl.Buffered(1)),
            pl.BlockSpec((None, d, d_ff), lambda i: (layer, 0, 0), pipeline_mode=pl.Buffered(1)),
            pl.BlockSpec((None, d_ff, d), lambda i: (layer, 0, 0), pipeline_mode=pl.Buffered(1)),
        ],
        out_specs=pl.BlockSpec((tm, d), lambda i: (i, 0)),
        compiler_params=_params("parallel"),
        name="ffn",
    )(tok, modtab, norm_g, wg, wu, wd)


def _inproj_kernel(x_ref, mod_ref, g_ref, w_ref, b_ref, cos_ref, sin_ref,
                   pa_ref, pb_ref, pc_ref, ps_ref):
    x = x_ref[...]
    mod = mod_ref[...]
    g = g_ref[...]
    u = _rms(x, g[2:3]) * (1.0 + mod[4:5]) + mod[3:4]
    p = _dot(u, w_ref[...]) + b_ref[...]
    qk_w = A_OUT + A_KV
    qk = p[:, 0:qk_w]
    reps = qk_w // LANES
    cos = jnp.concatenate([cos_ref[...]] * reps, axis=1)
    sin = jnp.concatenate([sin_ref[...]] * reps, axis=1)
    half = ROPE_AXIS // 2
    first_half = (_iota(qk.shape, 1) & (ROPE_AXIS - 1)) < half
    partner = jnp.where(first_half, pltpu.roll(qk, qk_w - half, 1), pltpu.roll(qk, half, 1))
    rot = qk * cos + partner * sin
    pa_ref[:, 0:A_OUT] = rot[:, 0:A_OUT] * (A_HEAD_DIM ** -0.5 * LOG2_E)
    pa_ref[:, A_OUT:qk_w] = rot[:, A_OUT:qk_w]
    pa_ref[:, qk_w:PA_W] = p[:, qk_w:PA_W]
    pb_ref[...] = p[:, PA_W:PA_W + PB_W]
    pc_ref[...] = p[:, PA_W + PB_W:PA_W + PB_W + PC_W]
    ps_ref[...] = p[:, PA_W + PB_W + PC_W:]


def _inproj(tok, modtab, norm_g, w_in, b_in, rope_cos, rope_sin, geom, layer):
    tm = geom.tm
    d = tok.shape[1]
    n_tok = tok.shape[0]
    width = w_in.shape[2]
    outs = [jax.ShapeDtypeStruct((n_tok, w), F32) for w in (PA_W, PB_W, PC_W, PS_W)]
    return pl.pallas_call(
        _inproj_kernel,
        out_shape=outs,
        grid=(geom.tiles,),
        in_specs=[
            pl.BlockSpec((tm, d), lambda i: (i, 0)),
            pl.BlockSpec((None, None, N_MOD, d), lambda i: (layer, geom.mod_row(i), 0, 0)),
            pl.BlockSpec((None, 6, d), lambda i: (layer, 0, 0)),
            pl.BlockSpec((None, d, width), lambda i: (layer, 0, 0)),
            pl.BlockSpec((None, 1, width), lambda i: (layer, 0, 0)),
            pl.BlockSpec((tm, LANES), lambda i: (geom.rope_block(i), 0)),
            pl.BlockSpec((tm, LANES), lambda i: (geom.rope_block(i), 0)),
        ],
        out_specs=[pl.BlockSpec((tm, w), lambda i: (i, 0)) for w in (PA_W, PB_W, PC_W, PS_W)],
        compiler_params=_params("parallel"),
        name="inproj",
    )(tok, modtab, norm_g, w_in, b_in, rope_cos, rope_sin)


def _attn_kernel(q_ref, k0_ref, k1_ref, k2_ref, k3_ref, v0_ref, v1_ref, v2_ref, v3_ref, kc_ref, vc_ref,
                 sink_ref, o_ref, *, ctx_steps, seq_len):
    n = pl.program_id(1) - ctx_steps
    qb = ATT_QUERIES
    n_win = 4 * ATT_BLOCK
    hd = A_HEAD_DIM
    k_abs = n * qb - WINDOW + _iota((n_win, qb), 0)
    q_abs = n * qb + _iota((n_win, qb), 1)
    hi = jnp.where(n >= 0, seq_len, 0)
    mask = (jnp.abs(q_abs - k_abs) <= WINDOW) & (k_abs >= 0) & (k_abs < hi)
    q_t = q_ref[...].T
    k_all = jnp.concatenate([k0_ref[...], k1_ref[...], k2_ref[...], k3_ref[...], kc_ref[...]], axis=0)
    v_t = jnp.concatenate([v0_ref[...], v1_ref[...], v2_ref[...], v3_ref[...], vc_ref[...]], axis=0).T
    zero = jnp.zeros((hd, qb), F32)
    cols, sinks = [], []
    for h in range(A_HEADS):
        q_h = q_t[h * hd:(h + 1) * hd]
        cols.append(jnp.concatenate([q_h, zero] if h // A_REP == 0 else [zero, q_h], axis=0))
        sinks.append(jnp.broadcast_to(sink_ref[h:h + 1, 0:1] * LOG2_E, (1, qb)))
    sk = jnp.concatenate(sinks, axis=1)
    s = _dot(k_all, jnp.concatenate(cols, axis=1))
    s_win = jnp.where(jnp.concatenate([mask] * A_HEADS, axis=1), s[:n_win], -jnp.inf)
    s_ctx = s[n_win:]
    m = jnp.maximum(jnp.maximum(jnp.max(s_win, axis=0, keepdims=True),
                                jnp.max(s_ctx, axis=0, keepdims=True)), sk)
    p = jnp.concatenate([jnp.exp2((s_win - m).astype(BF16)), jnp.exp2((s_ctx - m).astype(BF16))], axis=0)
    ones = jnp.ones((2 * SUBLANES, v_t.shape[1]), BF16)
    per_group = A_REP * qb
    outs = []
    for g in range(A_KV_HEADS):
        lanes = slice(g * per_group, (g + 1) * per_group)
        v_aug = jnp.concatenate([v_t[g * hd:(g + 1) * hd].astype(BF16), ones], axis=0)
        o_aug = jnp.dot(v_aug, p[:, lanes], preferred_element_type=F32)
        den = o_aug[hd:hd + 1] + jnp.exp2(sk[:, lanes] - m[:, lanes])
        o_g = o_aug[0:hd] / den
        outs += [o_g[:, r * qb:(r + 1) * qb] for r in range(A_REP)]
    o_ref[...] = jnp.concatenate(outs, axis=0).T


def _attn(pa, sink_rows, batch, seq, ctx_len):
    n_tok = pa.shape[0]
    qb = ATT_QUERIES
    blk = ATT_BLOCK
    per_q = qb // blk
    ctx_steps = ctx_len // qb
    seq_steps = seq // qb
    seq_blocks = seq // blk
    lat0 = batch * (ctx_len // blk)
    k_col = A_OUT // LANES
    v_col = k_col + 1

    def q_map(b, j):
        return (jnp.where(j < ctx_steps, b * ctx_steps + j, batch * ctx_steps + b * seq_steps + j - ctx_steps), 0)

    def kv_map(i, col):
        def f(b, j):
            n = jnp.clip((j - ctx_steps) * per_q - 1 + i, 0, seq_blocks - 1)
            return (lat0 + b * seq_blocks + n, col)
        return f

    kv_specs = [pl.BlockSpec((blk, LANES), kv_map(i, col))
                for col in (k_col, v_col) for i in range(per_q + 2)]
    return pl.pallas_call(
        functools.partial(_attn_kernel, ctx_steps=ctx_steps, seq_len=seq),
        out_shape=jax.ShapeDtypeStruct((n_tok, A_OUT), F32),
        grid=(batch, ctx_steps + seq_steps),
        in_specs=[pl.BlockSpec((qb, A_OUT), q_map)] + kv_specs + [
            pl.BlockSpec((ctx_len, LANES), lambda b, j: (b, k_col)),
            pl.BlockSpec((ctx_len, LANES), lambda b, j: (b, v_col)),
            pl.BlockSpec((SUBLANES, LANES), lambda b, j: (0, 0)),
        ],
        out_specs=pl.BlockSpec((qb, A_OUT), q_map),
        compiler_params=_params("parallel", "parallel"),
        name="attn",
    )(*([pa] * (2 * (per_q + 2) + 3)), sink_rows)


def _seq_block(b, j, reverse, batch, ctx_blocks, seq_blocks):
    jc = jnp.where(reverse, ctx_blocks - 1 - j, j)
    jl = jnp.where(reverse, seq_blocks - 1 - (j - ctx_blocks), j - ctx_blocks)
    return jnp.where(j < ctx_blocks, b * ctx_blocks + jc, batch * ctx_blocks + b * seq_blocks + jl)


def _gla_kernel(pbf_ref, psf_ref, pbb_ref, psb_ref, wa_ref, ba_ref, of_ref, ob_ref, st_ref):
    @pl.when(pl.program_id(1) == 0)
    def _():
        st_ref[...] = jnp.zeros_like(st_ref)

    _round_robin(_gla_chain(pbf_ref, psf_ref, wa_ref.at[0], ba_ref.at[0], of_ref, st_ref.at[0], reverse=False),
                 _gla_chain(pbb_ref, psb_ref, wa_ref.at[1], ba_ref.at[1], ob_ref, st_ref.at[1], reverse=True))


def _gla_chain(pb_ref, ps_ref, wa_ref, ba_ref, o_ref, st_ref, *, reverse):
    blk = SEQ_BLOCK
    ch = GLA_CHUNK
    half = ch // 2
    assert half == SUBLANES
    n_ch = blk // ch
    shift = ch.bit_length() - 1
    q = pb_ref[:, 0:B_KW] * (B_DK ** -0.5)
    k = pb_ref[:, B_KW:2 * B_KW]
    v = pb_ref[:, 2 * B_KW:2 * B_KW + B_VW]
    log_a = _log_sigmoid(_dot(ps_ref[...], wa_ref[...]) + ba_ref[...]) * (1.0 / GLA_TAU)

    r = _iota((blk, blk), 0)
    c = _iota((blk, blk), 1)
    same = (r >> shift) == (c >> shift)
    tri = same & ((c >= r) if reverse else (c <= r))
    bcum = _dot_sel(tri.astype(F32), log_a)
    btot = _dot_sel(same.astype(F32), log_a)
    q_in = q * jnp.exp(bcum)
    k_out = k * jnp.exp(btot - bcum)
    v_t = v.T

    row_chunk = _iota((blk, B_KW), 0) >> shift
    t_full = _iota((ch, B_KW), 0)
    t_half = _iota((half, B_KW), 0)
    head_expand = _div(_iota((B_KW, B_VW), 0), B_DK) == _div(_iota((B_KW, B_VW), 1), B_DV)
    head_expand = head_expand.astype(BF16)
    state_mask = _div(_iota((B_VW, B_KW), 0), B_DV) == _div(_iota((B_VW, B_KW), 1), B_DK)
    full_src = range(half, ch) if reverse else range(half)
    half_src = range(half) if reverse else range(half, ch)
    h0 = 0 if reverse else half
    yield

    st = st_ref[...]
    outs = [None] * n_ch
    for ci in (range(n_ch - 1, -1, -1) if reverse else range(n_ch)):
        lo = ci * ch
        q_c, k_c, b_c, v_c = q[lo:lo + ch], k[lo:lo + ch], bcum[lo:lo + ch], v[lo:lo + ch]
        q_h, b_h = q_c[h0:h0 + half], b_c[h0:h0 + half]
        pieces = []
        for s in full_src:
            seen = (t_full <= s) if reverse else (t_full >= s)
            arg = jnp.where(seen, b_c - b_c[s:s + 1], -jnp.inf)
            pieces.append(q_c * k_c[s:s + 1] * jnp.exp(arg))
        for s in half_src:
            seen = (t_half <= s - h0) if reverse else (t_half >= s - h0)
            arg = jnp.where(seen, b_h - b_c[s:s + 1], -jnp.inf)
            pieces.append(q_h * k_c[s:s + 1] * jnp.exp(arg))
        w = _dot(jnp.concatenate(pieces, axis=0), head_expand)
        o_c = _dot_nt(q_in[lo:lo + ch], st)
        o_halves = [o_c[0:half], o_c[half:ch]]
        row = 0
        for s in full_src:
            for part in range(2):
                o_halves[part] = o_halves[part] + w[row:row + half] * v_c[s:s + 1]
                row += half
        for s in half_src:
            part = h0 // half
            o_halves[part] = o_halves[part] + w[row:row + half] * v_c[s:s + 1]
            row += half
        outs[ci] = jnp.concatenate(o_halves, axis=0)
        st_loc = _dot(v_t, jnp.where(row_chunk == ci, k_out, 0.0))
        st = st * jnp.exp(btot[lo:lo + 1]) + jnp.where(state_mask, st_loc, 0.0)
        yield
    st_ref[...] = st
    o_ref[...] = jnp.concatenate(outs, axis=0)


def _gla(pb, ps, wa_pad, ba, batch, seq, ctx_len, layer):
    n_tok = pb.shape[0]
    blk = SEQ_BLOCK
    ctx_blocks, seq_blocks = ctx_len // blk, seq // blk

    def chain_specs(reverse):
        def blk_map(b, j):
            return (_seq_block(b, j, reverse, batch, ctx_blocks, seq_blocks), 0)
        return [pl.BlockSpec((blk, 2 * B_KW + B_VW), blk_map), pl.BlockSpec((blk, PS_W), blk_map)], \
            pl.BlockSpec((blk, B_VW), blk_map)

    in_f, out_f = chain_specs(False)
    in_b, out_b = chain_specs(True)
    out = jax.ShapeDtypeStruct((n_tok, B_VW), F32)
    return pl.pallas_call(
        _gla_kernel,
        out_shape=[out, out],
        grid=(batch, ctx_blocks + seq_blocks),
        in_specs=in_f + in_b + [
            pl.BlockSpec((None, 2, PS_W, B_KW), lambda b, j: (layer, 0, 0, 0)),
            pl.BlockSpec((None, 2, 1, B_KW), lambda b, j: (layer, 0, 0, 0)),
        ],
        out_specs=[out_f, out_b],
        scratch_shapes=[pltpu.VMEM((2, B_VW, B_KW), F32)],
        compiler_params=_params("parallel", "arbitrary"),
        name="gla",
    )(pb, ps, pb, ps, wa_pad, ba)


def _lane_scan(x, op, identity, reverse):
    n = x.shape[1]
    lane = _iota(x.shape, 1)
    k = 1
    while k < n:
        shifted = pltpu.roll(x, n - k if reverse else k, 1)
        inside = (lane < n - k) if reverse else (lane >= k)
        x = op(x, jnp.where(inside, shifted, identity))
        k *= 2
        yield
    return x


def _round_robin(*chains):
    live = list(range(len(chains)))
    results = [None] * len(chains)
    while live:
        for i in list(live):
            try:
                next(chains[i])
            except StopIteration as stop:
                results[i] = stop.value
                live.remove(i)
    return results


def _mlstm_kernel(pcf_ref, prevf_ref, nextf_ref, psf_ref, pcb_ref, prevb_ref, nextb_ref, psb_ref,
                  cw_ref, sel_ref, fb_ref, of_ref, ob_ref, c_ref, n_ref, m_ref, rows_ref, stat_ref,
                  *, ctx_blocks, seq_blocks):
    j = pl.program_id(1)

    @pl.when(j == 0)
    def _():
        c_ref[...] = jnp.zeros_like(c_ref)
        n_ref[...] = jnp.zeros_like(n_ref)
        m_ref[...] = jnp.zeros_like(m_ref)

    in_ctx = j < ctx_blocks
    n_seq = jnp.where(in_ctx, ctx_blocks, seq_blocks)
    jj = jnp.where(in_ctx, j, j - ctx_blocks)
    _, _, qkv_f, qkv_b = _round_robin(
        _mlstm_gates(psf_ref, sel_ref.at[0], fb_ref.at[0], rows_ref.at[0], stat_ref.at[0], rev=False),
        _mlstm_gates(psb_ref, sel_ref.at[1], fb_ref.at[1], rows_ref.at[1], stat_ref.at[1], rev=True),
        _mlstm_conv(pcf_ref, prevf_ref, nextf_ref, cw_ref, pos=jj, n_seq=n_seq),
        _mlstm_conv(pcb_ref, prevb_ref, nextb_ref, cw_ref, pos=n_seq - 1 - jj, n_seq=n_seq))
    _round_robin(
        _mlstm_step(qkv_f, rows_ref.at[0], stat_ref.at[0], of_ref, c_ref.at[0], n_ref.at[0], m_ref.at[0],
                    rev=False),
        _mlstm_step(qkv_b, rows_ref.at[1], stat_ref.at[1], ob_ref, c_ref.at[1], n_ref.at[1], m_ref.at[1],
                    rev=True))


def _mlstm_conv(pc_ref, prev_ref, next_ref, cw_ref, *, pos, n_seq):
    blk = SEQ_BLOCK
    half = blk // 2
    prev_row = jnp.where(pos > 0, prev_ref[SUBLANES - 1:SUBLANES, :], 0.0)
    next_row = jnp.where(pos < n_seq - 1, next_ref[0:1, :], 0.0)
    row = _iota((half, LANES), 0)
    tiles = []
    for c0 in range(0, 2 * C_W, LANES):
        lanes = slice(c0, c0 + LANES)
        halves = []
        for r0 in (0, half):
            x = pc_ref[r0:r0 + half, lanes]
            above = prev_row[:, lanes] if r0 == 0 else pc_ref[r0 - 1:r0, lanes]
            below = next_row[:, lanes] if r0 + half == blk else pc_ref[r0 + half:r0 + half + 1, lanes]
            x_prev = jnp.where(row == 0, above, pltpu.roll(x, 1, 0))
            x_next = jnp.where(row == half - 1, below, pltpu.roll(x, half - 1, 0))
            halves.append(_silu(x_prev * cw_ref[0:1, lanes] + x * cw_ref[1:2, lanes]
                                + x_next * cw_ref[2:3, lanes]))
            yield
        tiles.append(jnp.concatenate(halves, axis=0))
    q = jnp.concatenate(tiles[:C_W // LANES], axis=1)
    k = jnp.concatenate(tiles[C_W // LANES:], axis=1) * (C_DH ** -0.5)
    v = pc_ref[:, 2 * C_W:3 * C_W]
    head_of_lane = _div(_iota((blk, C_W), 1), C_DH)
    qk = []
    for h in range(C_HEADS):
        qk.append(_dot_nt(q, jnp.where(head_of_lane == h, k, 0.0)))
        yield
    return q, k, v, qk


def _mlstm_gates(ps_ref, sel_ref, fb_ref, rows_ref, stat_ref, *, rev):
    sub, blk, nh = SUBLANES, SEQ_BLOCK, C_HEADS
    g = _sel_dot_nt(sel_ref[...], ps_ref[...])
    li = g[0:sub]
    lf = jnp.where(_iota((sub, blk), 0) < nh, _log_sigmoid(g[sub:2 * sub] + fb_ref[:, 0:1]), 0.0)
    yield
    src_tok = _iota((blk, blk), 0)
    dst_tok = _iota((blk, blk), 1)
    upto = (src_tok >= dst_tok) if rev else (src_tok <= dst_tok)
    bcum = _dot_sel_rhs(lf, upto.astype(F32))
    yield
    btot = jnp.sum(lf, axis=1, keepdims=True)
    u = li - bcum
    u_max = yield from _lane_scan(u, jnp.maximum, -jnp.inf, rev)
    g_end = btot - bcum + li
    m_loc = jnp.max(g_end, axis=1, keepdims=True)
    rows_ref[...] = jnp.concatenate([u, u_max, bcum, jnp.exp(g_end - m_loc)], axis=0)
    stat_ref[...] = jnp.concatenate([jnp.broadcast_to(btot, (sub, LANES)),
                                     jnp.broadcast_to(m_loc, (sub, LANES))], axis=0)


def _mlstm_step(qkv, rows_ref, stat_ref, o_ref, c_ref, n_ref, m_ref, *, rev):
    blk = SEQ_BLOCK
    nh, dh = C_HEADS, C_DH
    sub = SUBLANES
    q, k, v, qk = qkv
    u = rows_ref[0:sub, :]
    u_max = rows_ref[sub:2 * sub, :]
    bcum = rows_ref[2 * sub:3 * sub, :]
    w_end = rows_ref[3 * sub:4 * sub, :]
    btot = stat_ref[0:sub, 0:1]
    m_loc = stat_ref[sub:2 * sub, 0:1]
    yield

    m_prev = m_ref[:, 0:1]
    big_m = jnp.maximum(u_max, m_prev)
    a_t = jnp.exp(m_prev - big_m)
    neg_m_t = -(bcum + big_m)
    m_new = jnp.maximum(btot + m_prev, m_loc)
    a_st = jnp.exp(btot + m_prev - m_new)
    e_st = jnp.exp(m_loc - m_new)

    pad = jnp.zeros((LANES - 4 * sub, blk), F32)
    yield
    cols = jnp.concatenate([big_m, a_t, neg_m_t, w_end, pad], axis=0).T
    yield
    width = 3 * C_W
    src_row = _iota((LANES, width), 0)
    lane = _iota((LANES, width), 1)
    src = sub * (1 + _div(lane, C_W)) + _div(lane & (C_W - 1), dh)
    wide = _dot_sel_rhs(cols, (src_row == src).astype(F32), parts=2)
    a_all = wide[:, 0:C_W]
    floor_all = jnp.exp(wide[:, C_W:2 * C_W])
    w_all = wide[:, 2 * C_W:]

    c_t = c_ref[...]
    n_row = n_ref[0:1, :]
    rr = _iota((blk, blk), 0)
    cc = _iota((blk, blk), 1)
    causal = (rr <= cc) if rev else (cc <= rr)
    head_of_lane = _div(_iota((blk, C_W), 1), dh)
    num = a_all * _dot(q, c_t)
    row_sum = None
    for h in range(nh):
        in_head = head_of_lane == h
        e = jnp.where(causal, u[h:h + 1, :] - cols[:, h:h + 1], -jnp.inf)
        sc = (qk[h] * jnp.exp(e)).astype(BF16)
        num = num + jnp.dot(sc, jnp.where(in_head, v, 0.0).astype(BF16), preferred_element_type=F32)
        rs = jnp.dot(sc, in_head.astype(BF16), preferred_element_type=F32)
        row_sum = rs if row_sum is None else row_sum + rs
        yield
    same_head = _div(_iota((C_W, C_W), 0), dh) == _div(_iota((C_W, C_W), 1), dh)
    den = a_all * _dot(q * n_row, same_head) + row_sum
    o_ref[...] = num / jnp.maximum(jnp.abs(den), floor_all)
    yield

    head8 = _div(_iota((sub, C_W), 1), dh) == _iota((sub, C_W), 0)
    a_lane = jnp.sum(jnp.where(head8, a_st, 0.0), axis=0, keepdims=True)
    e_lane = jnp.sum(jnp.where(head8, e_st, 0.0), axis=0, keepdims=True)
    c_loc = _dot(k.T, v * w_all)
    c_ref[...] = a_lane * c_t + jnp.where(same_head, e_lane * c_loc, 0.0)
    n_ref[0:1, :] = a_lane * n_row + e_lane * jnp.sum(k * w_all, axis=0, keepdims=True)
    m_ref[...] = jnp.broadcast_to(m_new, m_ref.shape)


def _mlstm(pc, ps, conv_w, gate_sel, f_bias_rows, batch, seq, ctx_len, layer):
    n_tok = pc.shape[0]
    blk = SEQ_BLOCK
    ctx_blocks, seq_blocks = ctx_len // blk, seq // blk
    rows8 = blk // SUBLANES
    last8 = n_tok // SUBLANES - 1

    def chain_specs(reverse):
        def blk_of(b, j):
            return _seq_block(b, j, reverse, batch, ctx_blocks, seq_blocks)
        return [
            pl.BlockSpec((blk, 3 * C_W), lambda b, j: (blk_of(b, j), 0)),
            pl.BlockSpec((SUBLANES, 2 * C_W), lambda b, j: (jnp.maximum(blk_of(b, j) * rows8 - 1, 0), 0)),
            pl.BlockSpec((SUBLANES, 2 * C_W), lambda b, j: (jnp.minimum((blk_of(b, j) + 1) * rows8, last8), 0)),
            pl.BlockSpec((blk, PS_W), lambda b, j: (blk_of(b, j), 0)),
        ], pl.BlockSpec((blk, C_W), lambda b, j: (blk_of(b, j), 0))

    in_f, out_f = chain_specs(False)
    in_b, out_b = chain_specs(True)
    out = jax.ShapeDtypeStruct((n_tok, C_W), F32)
    return pl.pallas_call(
        functools.partial(_mlstm_kernel, ctx_blocks=ctx_blocks, seq_blocks=seq_blocks),
        out_shape=[out, out],
        grid=(batch, ctx_blocks + seq_blocks),
        in_specs=in_f + in_b + [
            pl.BlockSpec((None, 3, 2 * C_W), lambda b, j: (layer, 0, 0)),
            pl.BlockSpec((2, 2 * SUBLANES, PS_W), lambda b, j: (0, 0, 0)),
            pl.BlockSpec((None, 2, SUBLANES, LANES), lambda b, j: (layer, 0, 0, 0)),
        ],
        out_specs=[out_f, out_b],
        scratch_shapes=[pltpu.VMEM((2, C_W, C_W), F32), pltpu.VMEM((2, SUBLANES, C_W), F32),
                        pltpu.VMEM((2, SUBLANES, LANES), F32), pltpu.VMEM((2, 4 * SUBLANES, blk), F32),
                        pltpu.VMEM((2, 2 * SUBLANES, LANES), F32)],
        compiler_params=_params("parallel", "arbitrary"),
        name="mlstm",
    )(pc, pc, pc, ps, pc, pc, pc, ps, conv_w, gate_sel, f_bias_rows)


def _head_layernorm(o, g, width):
    n = o.shape[1]
    avg = jnp.where(_div(_iota((n, n), 0), width) == _div(_iota((n, n), 1), width), 1.0 / width, 0.0)
    mu = _dot_sel_rhs(o, avg, parts=2)
    cen = o - mu
    var = _dot_sel_rhs(cen * cen, avg, parts=2)
    return cen * lax.rsqrt(var + EPS) * g


def _out_kernel(x_ref, mod_ref, g_ref, ya_ref, gb_ref, obf_ref, obb_ref, gc_ref, ocf_ref, ocb_ref,
                gng_ref, mng_ref, w_ref, o_ref):
    y_b = _silu(gb_ref[...]) * _head_layernorm(obf_ref[...] + obb_ref[...], gng_ref[...], B_DV)
    y_c = _sigmoid(gc_ref[...]) * _head_layernorm(ocf_ref[...] + ocb_ref[...], mng_ref[...], C_DH)
    y = (_dot(ya_ref[...], w_ref[0:A_OUT, :])
         + _dot(y_b, w_ref[A_OUT:A_OUT + B_VW, :])
         + _dot(y_c, w_ref[A_OUT + B_VW:, :]))
    o_ref[...] = x_ref[...] + mod_ref[5:6, :] * _rms(y, g_ref[3:4, :])


def _out(tok, modtab, norm_g, ya, pb, ob_f, ob_b, pc, oc_f, oc_b, gla_norm, mlstm_norm, w_out, geom, layer):
    tm = geom.tm
    n_tok, d = tok.shape
    gb_col = (2 * B_KW + B_VW) // B_VW
    gc_col = 3
    return pl.pallas_call(
        _out_kernel,
        out_shape=jax.ShapeDtypeStruct((n_tok, d), F32),
        grid=(geom.tiles,),
        in_specs=[
            pl.BlockSpec((tm, d), lambda i: (i, 0)),
            pl.BlockSpec((None, None, N_MOD, d), lambda i: (layer, geom.mod_row(i), 0, 0)),
            pl.BlockSpec((None, 6, d), lambda i: (layer, 0, 0)),
            pl.BlockSpec((tm, A_OUT), lambda i: (i, 0)),
            pl.BlockSpec((tm, B_VW), lambda i: (i, gb_col)),
            pl.BlockSpec((tm, B_VW), lambda i: (i, 0)),
            pl.BlockSpec((tm, B_VW), lambda i: (i, 0)),
            pl.BlockSpec((tm, C_W), lambda i: (i, gc_col)),
            pl.BlockSpec((tm, C_W), lambda i: (i, 0)),
            pl.BlockSpec((tm, C_W), lambda i: (i, 0)),
            pl.BlockSpec((None, 1, B_VW), lambda i: (layer, 0, 0)),
            pl.BlockSpec((None, 1, C_W), lambda i: (layer, 0, 0)),
            pl.BlockSpec((None, d, d), lambda i: (layer, 0, 0)),
        ],
        out_specs=pl.BlockSpec((tm, d), lambda i: (i, 0)),
        compiler_params=_params("parallel"),
        name="out",
    )(tok, modtab, norm_g, ya, pb, ob_f, ob_b, pc, oc_f, oc_b, gla_norm, mlstm_norm, w_out)


def _rope_tables(seq, tm):
    pos = jnp.arange(seq)
    half = ROPE_AXIS // 2
    inv_freq = ROPE_BASE ** (-jnp.arange(half, dtype=F32) / half)
    ang_r = (pos // GRID_W).astype(F32)[:, None] * inv_freq[None, :]
    ang_c = (pos % GRID_W).astype(F32)[:, None] * inv_freq[None, :]
    cos = jnp.concatenate([jnp.cos(ang_r)] * 2 + [jnp.cos(ang_c)] * 2, axis=1)
    sin = jnp.concatenate([-jnp.sin(ang_r), jnp.sin(ang_r), -jnp.sin(ang_c), jnp.sin(ang_c)], axis=1)
    reps = LANES // A_HEAD_DIM
    cos = jnp.concatenate([jnp.tile(cos, (1, reps)), jnp.ones((tm, LANES), F32)], axis=0)
    sin = jnp.concatenate([jnp.tile(sin, (1, reps)), jnp.zeros((tm, LANES), F32)], axis=0)
    return cos, sin


def _regroup_in(w):
    a_end = PA_W + PB_W
    alpha = slice(a_end, a_end + 2 * GLA_RANK)
    c_qkv = slice(alpha.stop, alpha.stop + 3 * C_W)
    c_gate = slice(c_qkv.stop, c_qkv.stop + 4 * C_HEADS)
    c_og = slice(c_gate.stop, c_gate.stop + C_W)
    pad = jnp.zeros(w.shape[:-1] + (PS_W - 2 * GLA_RANK - 4 * C_HEADS,), w.dtype)
    return jnp.concatenate([w[..., :a_end], w[..., c_qkv], w[..., c_og], w[..., alpha], w[..., c_gate], pad],
                           axis=-1)


def _pick_tile(n_ctx, seq, largest):
    tm = largest
    while tm >= LANES:
        if n_ctx % tm == 0 and seq % tm == 0:
            return tm
        tm //= 2
    raise ValueError("token counts must be multiples of 128")


def kernel(x, c, ctx, c_ctx, ada_w, ada_b, norm_g, ffn1_wg, ffn1_wu, ffn1_wd, ffn2_wg, ffn2_wu, ffn2_wd,
           w_in, b_in, w_out, attn_sink, gla_w_alpha, gla_b_alpha, gla_norm, mlstm_conv, mlstm_f_bias,
           mlstm_norm):
    batch, seq, d = x.shape
    ctx_len = ctx.shape[1]
    n_l = ada_w.shape[0]
    assert seq % SEQ_BLOCK == 0 and ctx_len % SEQ_BLOCK == 0
    geom = _Geom(batch, seq, ctx_len, _pick_tile(batch * ctx_len, seq, TOKEN_TILE))
    geom_ffn = _Geom(batch, seq, ctx_len, _pick_tile(batch * ctx_len, seq, FFN_TOKEN_TILE))

    rows = -(-(batch + 1) // SUBLANES) * SUBLANES
    cc = jnp.concatenate([c, c_ctx[None, :], jnp.zeros((rows - batch - 1, d), F32)], axis=0)
    modtab = _ada_table(cc, ada_w, ada_b)

    rope_cos, rope_sin = _rope_tables(seq, geom.tm)
    w_in_r = _regroup_in(w_in.astype(BF16))
    b_in_r = _regroup_in(b_in)[:, None, :]
    wg1, wu1, wd1 = ffn1_wg.astype(BF16), ffn1_wu.astype(BF16), ffn1_wd.astype(BF16)
    wg2, wu2, wd2 = ffn2_wg.astype(BF16), ffn2_wu.astype(BF16), ffn2_wd.astype(BF16)
    w_out_b = w_out.astype(BF16)

    wa_pad = jnp.zeros((n_l, 2, PS_W, B_KW), F32)
    for dirn in range(2):
        wa_pad = wa_pad.at[:, dirn, dirn * GLA_RANK:(dirn + 1) * GLA_RANK, :].set(gla_w_alpha[:, dirn])
    wa_pad = wa_pad.astype(BF16)
    ba = gla_b_alpha[:, :, None, :]

    nh = C_HEADS
    out_idx = jnp.arange(2 * SUBLANES)[:, None]
    src_idx = jnp.arange(PS_W)[None, :]
    head_idx = out_idx % SUBLANES
    gate_sel = jnp.stack([
        (head_idx < nh) & (src_idx == PS_GATE0 + 2 * nh * dirn + nh * (out_idx // SUBLANES) + head_idx)
        for dirn in range(2)]).astype(F32)
    f_bias_rows = jnp.zeros((n_l, 2, SUBLANES), F32).at[:, :, 0:nh].set(mlstm_f_bias)
    f_bias_rows = jnp.broadcast_to(f_bias_rows[:, :, :, None], (n_l, 2, SUBLANES, LANES))

    sink_rows = jnp.broadcast_to(attn_sink[:, :, None], (n_l, A_HEADS, LANES))
    gla_g = gla_norm[:, None, :]
    mlstm_g = mlstm_norm[:, None, :]

    tok = jnp.concatenate([ctx.reshape(batch * ctx_len, d), x.reshape(batch * seq, d)], axis=0)
    for layer in range(n_l):
        last = layer == n_l - 1
        tok = _ffn(tok, modtab, norm_g, wg1, wu1, wd1, geom_ffn, layer, 0, 0)
        pa, pb, pc, ps = _inproj(tok, modtab, norm_g, w_in_r, b_in_r, rope_cos, rope_sin, geom, layer)
        ya = _attn(pa, sink_rows[layer], batch, seq, ctx_len)
        ob_f, ob_b = _gla(pb, ps, wa_pad, ba, batch, seq, ctx_len, layer)
        oc_f, oc_b = _mlstm(pc, ps, mlstm_conv, gate_sel, f_bias_rows, batch, seq, ctx_len, layer)
        tok = _out(tok, modtab, norm_g, ya, pb, ob_f, ob_b, pc, oc_f, oc_b, gla_g, mlstm_g, w_out_b, geom, layer)
        tok = _ffn(tok, modtab, norm_g, wg2, wu2, wd2, geom_ffn, layer, 6, 4, latent_only=last)
    return tok.reshape(batch, seq, d)
```

```python
import functools

import jax
import jax.numpy as jnp
from jax import lax
from jax.experimental import pallas as pl
from jax.experimental.pallas import tpu as pltpu

F32 = jnp.float32
BF16 = jnp.bfloat16

GRID_W = 64
A_HEAD_DIM = 64
A_HEADS = 8
A_KV_HEADS = 2
A_REP = A_HEADS // A_KV_HEADS
WINDOW = 128
ROPE_BASE = 10000.0
ROPE_AXIS = A_HEAD_DIM // 2
B_DK = 32
B_DV = 64
B_HEADS = 4
GLA_RANK = 16
GLA_TAU = 16.0
GLA_CHUNK = 16
C_DH = 64
C_HEADS = 4
A_OUT = A_HEADS * A_HEAD_DIM
A_KV = A_KV_HEADS * A_HEAD_DIM
B_KW = B_HEADS * B_DK
B_VW = B_HEADS * B_DV
C_W = C_HEADS * C_DH
FFN_RES = 0.5
N_MOD = 9
EPS = 1e-6
LOG2_E = 1.4426950408889634

PA_W = A_OUT + 2 * A_KV
PB_W = 2 * B_KW + 2 * B_VW
PC_W = 4 * C_W
PS_W = 128
PS_GATE0 = 2 * GLA_RANK

LANES = 128
SUBLANES = 8
SEQ_BLOCK = 256
ATT_BLOCK = WINDOW
ATT_QUERIES = 2 * ATT_BLOCK
FF_CHUNK = 256
TOKEN_TILE = 512
FFN_TOKEN_TILE = 1024
V7X_VMEM_BYTES = 64 * 1024 * 1024
VMEM_LIMIT = V7X_VMEM_BYTES * 7 // 8


def _sigmoid(x):
    return 1.0 / (1.0 + jnp.exp(-x))


def _silu(x):
    return x * _sigmoid(x)


def _log_sigmoid(x):
    return jnp.minimum(x, 0.0) - jnp.log(1.0 + jnp.exp(-jnp.abs(x)))


def _rms(x, g):
    return x * lax.rsqrt(jnp.mean(x * x, axis=-1, keepdims=True) + EPS) * g


def _dot(a, b):
    return jnp.dot(a.astype(BF16), b.astype(BF16), preferred_element_type=F32)


def _dot_nt(a, b):
    return lax.dot_general(a.astype(BF16), b.astype(BF16), (((1,), (1,)), ((), ())),
                           preferred_element_type=F32)


def _split(x, parts):
    out = []
    for _ in range(parts):
        t = x.astype(BF16)
        out.append(t)
        x = x - t.astype(F32)
    return out


def _dot_sel(sel, x, parts=3):
    sel = sel.astype(BF16)
    acc = None
    for t in _split(x, parts):
        y = jnp.dot(sel, t, preferred_element_type=F32)
        acc = y if acc is None else acc + y
    return acc


def _dot_sel_rhs(x, sel, parts=3):
    sel = sel.astype(BF16)
    acc = None
    for t in _split(x, parts):
        y = jnp.dot(t, sel, preferred_element_type=F32)
        acc = y if acc is None else acc + y
    return acc


def _sel_dot_nt(sel, x, parts=3):
    sel = sel.astype(BF16)
    acc = None
    for t in _split(x, parts):
        y = lax.dot_general(sel, t, (((1,), (1,)), ((), ())), preferred_element_type=F32)
        acc = y if acc is None else acc + y
    return acc


def _iota(shape, dim):
    return lax.broadcasted_iota(jnp.int32, shape, dim)


def _div(x, n):
    assert n & (n - 1) == 0
    return x >> (n.bit_length() - 1)


def _params(*semantics):
    return pltpu.CompilerParams(dimension_semantics=semantics, vmem_limit_bytes=VMEM_LIMIT)


def _ada_kernel(c_ref, w_ref, b_ref, o_ref):
    o_ref[...] = _dot(_silu(c_ref[...]), w_ref[...]) + b_ref[...]


def _ada_table(cc, ada_w, ada_b):
    n_l, d, _ = ada_w.shape
    r = cc.shape[0]
    out = pl.pallas_call(
        _ada_kernel,
        out_shape=jax.ShapeDtypeStruct((n_l, N_MOD, r, d), F32),
        grid=(n_l, N_MOD),
        in_specs=[
            pl.BlockSpec((r, d), lambda l, j: (0, 0)),
            pl.BlockSpec((None, d, d), lambda l, j: (l, 0, j)),
            pl.BlockSpec((None, 1, d), lambda l, j: (l, 0, j)),
        ],
        out_specs=pl.BlockSpec((None, None, r, d), lambda l, j: (l, j, 0, 0)),
        compiler_params=_params("parallel", "parallel"),
        name="ada",
    )(cc, ada_w, ada_b.reshape(n_l, 1, N_MOD * d))
    return out.transpose(0, 2, 1, 3)


class _Geom:
    def __init__(self, batch, seq, ctx_len, tm):
        self.batch, self.seq, self.ctx_len, self.tm = batch, seq, ctx_len, tm
        self.n_ctx = batch * ctx_len
        self.n_tok = batch * (ctx_len + seq)
        assert self.n_ctx % tm == 0 and seq % tm == 0
        self.ctx_tiles = self.n_ctx // tm
        self.tiles_per_batch = seq // tm
        self.tiles = self.n_tok // tm

    def mod_row(self, i):
        return jnp.where(i < self.ctx_tiles, self.batch, (i - self.ctx_tiles) // self.tiles_per_batch)

    def rope_block(self, i):
        return jnp.where(i < self.ctx_tiles, self.tiles_per_batch,
                         (i - self.ctx_tiles) % self.tiles_per_batch)


def _ffn_kernel(x_ref, mod_ref, g_ref, wg_ref, wu_ref, wd_ref, o_ref, *, mod_base, g_base):
    x = x_ref[...]
    mod = mod_ref[...]
    g = g_ref[...]
    shift = mod[mod_base:mod_base + 1]
    scale = mod[mod_base + 1:mod_base + 2]
    gate = mod[mod_base + 2:mod_base + 3]
    u = (_rms(x, g[g_base:g_base + 1]) * (1.0 + scale) + shift).astype(BF16)
    d_ff = wg_ref.shape[1]
    y = None
    for c0 in range(0, d_ff, FF_CHUNK):
        a = jnp.dot(u, wg_ref[:, c0:c0 + FF_CHUNK], preferred_element_type=F32)
        b = jnp.dot(u, wu_ref[:, c0:c0 + FF_CHUNK], preferred_element_type=F32)
        hcol = (_silu(a) * b).astype(BF16)
        part = jnp.dot(hcol, wd_ref[c0:c0 + FF_CHUNK, :], preferred_element_type=F32)
        y = part if y is None else y + part
    o_ref[...] = x + FFN_RES * gate * _rms(y, g[g_base + 1:g_base + 2])


def _ffn(tok, modtab, norm_g, wg, wu, wd, geom, layer, mod_base, g_base, latent_only=False):
    tm = geom.tm
    d = tok.shape[1]
    d_ff = wg.shape[2]
    assert d_ff % FF_CHUNK == 0
    first = geom.ctx_tiles if latent_only else 0
    n_tiles = geom.tiles - first
    return pl.pallas_call(
        functools.partial(_ffn_kernel, mod_base=mod_base, g_base=g_base),
        out_shape=jax.ShapeDtypeStruct((n_tiles * tm, d), F32),
        grid=(n_tiles,),
        in_specs=[
            pl.BlockSpec((tm, d), lambda i: (i + first, 0)),
            pl.BlockSpec((None, None, N_MOD, d), lambda i: (layer, geom.mod_row(i + first), 0, 0)),
            pl.BlockSpec((None, 6, d), lambda i: (layer, 0, 0)),
            pl.BlockSpec((None, d, d_ff), lambda i: (layer, 0, 0), pipeline_mode=pl.Buffered(1)),
            pl.BlockSpec((None, d, d_ff), lambda i: (layer, 0, 0), pipeline_mode=pl.Buffered(1)),
            pl.BlockSpec((None, d_ff, d), lambda i: (layer, 0, 0), pipeline_mode=pl.Buffered(1)),
        ],
        out_specs=pl.BlockSpec((tm, d), lambda i: (i, 0)),
        compiler_params=_params("parallel"),
        name="ffn",
    )(tok, modtab, norm_g, wg, wu, wd)


def _inproj_kernel(x_ref, mod_ref, g_ref, w_ref, b_ref, cos_ref, sin_ref,
                   pa_ref, pb_ref, pc_ref, ps_ref):
    x = x_ref[...]
    mod = mod_ref[...]
    g = g_ref[...]
    u = _rms(x, g[2:3]) * (1.0 + mod[4:5]) + mod[3:4]
    p = _dot(u, w_ref[...]) + b_ref[...]
    qk_w = A_OUT + A_KV
    qk = p[:, 0:qk_w]
    reps = qk_w // LANES
    cos = jnp.concatenate([cos_ref[...]] * reps, axis=1)
    sin = jnp.concatenate([sin_ref[...]] * reps, axis=1)
    half = ROPE_AXIS // 2
    first_half = (_iota(qk.shape, 1) & (ROPE_AXIS - 1)) < half
    partner = jnp.where(first_half, pltpu.roll(qk, qk_w - half, 1), pltpu.roll(qk, half, 1))
    rot = qk * cos + partner * sin
    pa_ref[:, 0:A_OUT] = rot[:, 0:A_OUT] * (A_HEAD_DIM ** -0.5 * LOG2_E)
    pa_ref[:, A_OUT:qk_w] = rot[:, A_OUT:qk_w]
    pa_ref[:, qk_w:PA_W] = p[:, qk_w:PA_W]
    pb_ref[...] = p[:, PA_W:PA_W + PB_W]
    pc_ref[...] = p[:, PA_W + PB_W:PA_W + PB_W + PC_W]
    ps_ref[...] = p[:, PA_W + PB_W + PC_W:]


def _inproj(tok, modtab, norm_g, w_in, b_in, rope_cos, rope_sin, geom, layer):
    tm = geom.tm
    d = tok.shape[1]
    n_tok = tok.shape[0]
    width = w_in.shape[2]
    outs = [jax.ShapeDtypeStruct((n_tok, w), F32) for w in (PA_W, PB_W, PC_W, PS_W)]
    return pl.pallas_call(
        _inproj_kernel,
        out_shape=outs,
        grid=(geom.tiles,),
        in_specs=[
            pl.BlockSpec((tm, d), lambda i: (i, 0)),
            pl.BlockSpec((None, None, N_MOD, d), lambda i: (layer, geom.mod_row(i), 0, 0)),
            pl.BlockSpec((None, 6, d), lambda i: (layer, 0, 0)),
            pl.BlockSpec((None, d, width), lambda i: (layer, 0, 0)),
            pl.BlockSpec((None, 1, width), lambda i: (layer, 0, 0)),
            pl.BlockSpec((tm, LANES), lambda i: (geom.rope_block(i), 0)),
            pl.BlockSpec((tm, LANES), lambda i: (geom.rope_block(i), 0)),
        ],
        out_specs=[pl.BlockSpec((tm, w), lambda i: (i, 0)) for w in (PA_W, PB_W, PC_W, PS_W)],
        compiler_params=_params("parallel"),
        name="inproj",
    )(tok, modtab, norm_g, w_in, b_in, rope_cos, rope_sin)


def _attn_kernel(q_ref, k0_ref, k1_ref, k2_ref, k3_ref, v0_ref, v1_ref, v2_ref, v3_ref, kc_ref, vc_ref,
                 sink_ref, o_ref, *, ctx_steps, seq_len):
    n = pl.program_id(1) - ctx_steps
    qb = ATT_QUERIES
    n_win = 4 * ATT_BLOCK
    hd = A_HEAD_DIM
    k_abs = n * qb - WINDOW + _iota((n_win, qb), 0)
    q_abs = n * qb + _iota((n_win, qb), 1)
    hi = jnp.where(n >= 0, seq_len, 0)
    mask = (jnp.abs(q_abs - k_abs) <= WINDOW) & (k_abs >= 0) & (k_abs < hi)
    q_t = q_ref[...].T
    k_all = jnp.concatenate([k0_ref[...], k1_ref[...], k2_ref[...], k3_ref[...], kc_ref[...]], axis=0)
    v_t = jnp.concatenate([v0_ref[...], v1_ref[...], v2_ref[...], v3_ref[...], vc_ref[...]], axis=0).T
    zero = jnp.zeros((hd, qb), F32)
    cols, sinks = [], []
    for h in range(A_HEADS):
        q_h = q_t[h * hd:(h + 1) * hd]
        cols.append(jnp.concatenate([q_h, zero] if h // A_REP == 0 else [zero, q_h], axis=0))
        sinks.append(jnp.broadcast_to(sink_ref[h:h + 1, 0:1] * LOG2_E, (1, qb)))
    sk = jnp.concatenate(sinks, axis=1)
    s = _dot(k_all, jnp.concatenate(cols, axis=1))
    s_win = jnp.where(jnp.concatenate([mask] * A_HEADS, axis=1), s[:n_win], -jnp.inf)
    s_ctx = s[n_win:]
    m = jnp.maximum(jnp.maximum(jnp.max(s_win, axis=0, keepdims=True),
                                jnp.max(s_ctx, axis=0, keepdims=True)), sk)
    p = jnp.concatenate([jnp.exp2((s_win - m).astype(BF16)), jnp.exp2((s_ctx - m).astype(BF16))], axis=0)
    ones = jnp.ones((2 * SUBLANES, v_t.shape[1]), BF16)
    per_group = A_REP * qb
    outs = []
    for g in range(A_KV_HEADS):
        lanes = slice(g * per_group, (g + 1) * per_group)
        v_aug = jnp.concatenate([v_t[g * hd:(g + 1) * hd].astype(BF16), ones], axis=0)
        o_aug = jnp.dot(v_aug, p[:, lanes], preferred_element_type=F32)
        den = o_aug[hd:hd + 1] + jnp.exp2(sk[:, lanes] - m[:, lanes])
        o_g = o_aug[0:hd] / den
        outs += [o_g[:, r * qb:(r + 1) * qb] for r in range(A_REP)]
    o_ref[...] = jnp.concatenate(outs, axis=0).T


def _attn(pa, sink_rows, batch, seq, ctx_len):
    n_tok = pa.shape[0]
    qb = ATT_QUERIES
    blk = ATT_BLOCK
    per_q = qb // blk
    ctx_steps = ctx_len // qb
    seq_steps = seq // qb
    seq_blocks = seq // blk
    lat0 = batch * (ctx_len // blk)
    k_col = A_OUT // LANES
    v_col = k_col + 1

    def q_map(b, j):
        return (jnp.where(j < ctx_steps, b * ctx_steps + j, batch * ctx_steps + b * seq_steps + j - ctx_steps), 0)

    def kv_map(i, col):
        def f(b, j):
            n = jnp.clip((j - ctx_steps) * per_q - 1 + i, 0, seq_blocks - 1)
            return (lat0 + b * seq_blocks + n, col)
        return f

    kv_specs = [pl.BlockSpec((blk, LANES), kv_map(i, col))
                for col in (k_col, v_col) for i in range(per_q + 2)]
    return pl.pallas_call(
        functools.partial(_attn_kernel, ctx_steps=ctx_steps, seq_len=seq),
        out_shape=jax.ShapeDtypeStruct((n_tok, A_OUT), F32),
        grid=(batch, ctx_steps + seq_steps),
        in_specs=[pl.BlockSpec((qb, A_OUT), q_map)] + kv_specs + [
            pl.BlockSpec((ctx_len, LANES), lambda b, j: (b, k_col)),
            pl.BlockSpec((ctx_len, LANES), lambda b, j: (b, v_col)),
            pl.BlockSpec((SUBLANES, LANES), lambda b, j: (0, 0)),
        ],
        out_specs=pl.BlockSpec((qb, A_OUT), q_map),
        compiler_params=_params("parallel", "parallel"),
        name="attn",
    )(*([pa] * (2 * (per_q + 2) + 3)), sink_rows)


def _seq_block(b, j, reverse, batch, ctx_blocks, seq_blocks):
    jc = jnp.where(reverse, ctx_blocks - 1 - j, j)
    jl = jnp.where(reverse, seq_blocks - 1 - (j - ctx_blocks), j - ctx_blocks)
    return jnp.where(j < ctx_blocks, b * ctx_blocks + jc, batch * ctx_blocks + b * seq_blocks + jl)


def _gla_kernel(pbf_ref, psf_ref, pbb_ref, psb_ref, wa_ref, ba_ref, of_ref, ob_ref, st_ref):
    @pl.when(pl.program_id(1) == 0)
    def _():
        st_ref[...] = jnp.zeros_like(st_ref)

    _round_robin(_gla_chain(pbf_ref, psf_ref, wa_ref.at[0], ba_ref.at[0], of_ref, st_ref.at[0], reverse=False),
                 _gla_chain(pbb_ref, psb_ref, wa_ref.at[1], ba_ref.at[1], ob_ref, st_ref.at[1], reverse=True))


def _gla_chain(pb_ref, ps_ref, wa_ref, ba_ref, o_ref, st_ref, *, reverse):
    blk = SEQ_BLOCK
    ch = GLA_CHUNK
    half = ch // 2
    assert half == SUBLANES
    n_ch = blk // ch
    shift = ch.bit_length() - 1
    q = pb_ref[:, 0:B_KW] * (B_DK ** -0.5)
    k = pb_ref[:, B_KW:2 * B_KW]
    v = pb_ref[:, 2 * B_KW:2 * B_KW + B_VW]
    log_a = _log_sigmoid(_dot(ps_ref[...], wa_ref[...]) + ba_ref[...]) * (1.0 / GLA_TAU)

    r = _iota((blk, blk), 0)
    c = _iota((blk, blk), 1)
    same = (r >> shift) == (c >> shift)
    tri = same & ((c >= r) if reverse else (c <= r))
    bcum = _dot_sel(tri.astype(F32), log_a)
    btot = _dot_sel(same.astype(F32), log_a)
    q_in = q * jnp.exp(bcum)
    k_out = k * jnp.exp(btot - bcum)
    v_t = v.T

    row_chunk = _iota((blk, B_KW), 0) >> shift
    t_full = _iota((ch, B_KW), 0)
    t_half = _iota((half, B_KW), 0)
    head_expand = _div(_iota((B_KW, B_VW), 0), B_DK) == _div(_iota((B_KW, B_VW), 1), B_DV)
    head_expand = head_expand.astype(BF16)
    state_mask = _div(_iota((B_VW, B_KW), 0), B_DV) == _div(_iota((B_VW, B_KW), 1), B_DK)
    full_src = range(half, ch) if reverse else range(half)
    half_src = range(half) if reverse else range(half, ch)
    h0 = 0 if reverse else half
    yield

    st = st_ref[...]
    outs = [None] * n_ch
    for ci in (range(n_ch - 1, -1, -1) if reverse else range(n_ch)):
        lo = ci * ch
        q_c, k_c, b_c, v_c = q[lo:lo + ch], k[lo:lo + ch], bcum[lo:lo + ch], v[lo:lo + ch]
        q_h, b_h = q_c[h0:h0 + half], b_c[h0:h0 + half]
        pieces = []
        for s in full_src:
            seen = (t_full <= s) if reverse else (t_full >= s)
            arg = jnp.where(seen, b_c - b_c[s:s + 1], -jnp.inf)
            pieces.append(q_c * k_c[s:s + 1] * jnp.exp(arg))
        for s in half_src:
            seen = (t_half <= s - h0) if reverse else (t_half >= s - h0)
            arg = jnp.where(seen, b_h - b_c[s:s + 1], -jnp.inf)
            pieces.append(q_h * k_c[s:s + 1] * jnp.exp(arg))
        w = _dot(jnp.concatenate(pieces, axis=0), head_expand)
        o_c = _dot_nt(q_in[lo:lo + ch], st)
        o_halves = [o_c[0:half], o_c[half:ch]]
        row = 0
        for s in full_src:
            for part in range(2):
                o_halves[part] = o_halves[part] + w[row:row + half] * v_c[s:s + 1]
                row += half
        for s in half_src:
            part = h0 // half
            o_halves[part] = o_halves[part] + w[row:row + half] * v_c[s:s + 1]
            row += half
        outs[ci] = jnp.concatenate(o_halves, axis=0)
        st_loc = _dot(v_t, jnp.where(row_chunk == ci, k_out, 0.0))
        st = st * jnp.exp(btot[lo:lo + 1]) + jnp.where(state_mask, st_loc, 0.0)
        yield
    st_ref[...] = st
    o_ref[...] = jnp.concatenate(outs, axis=0)


def _gla(pb, ps, wa_pad, ba, batch, seq, ctx_len, layer):
    n_tok = pb.shape[0]
    blk = SEQ_BLOCK
    ctx_blocks, seq_blocks = ctx_len // blk, seq // blk

    def chain_specs(reverse):
        def blk_map(b, j):
            return (_seq_block(b, j, reverse, batch, ctx_blocks, seq_blocks), 0)
        return [pl.BlockSpec((blk, 2 * B_KW + B_VW), blk_map), pl.BlockSpec((blk, PS_W), blk_map)], \
            pl.BlockSpec((blk, B_VW), blk_map)

    in_f, out_f = chain_specs(False)
    in_b, out_b = chain_specs(True)
    out = jax.ShapeDtypeStruct((n_tok, B_VW), F32)
    return pl.pallas_call(
        _gla_kernel,
        out_shape=[out, out],
        grid=(batch, ctx_blocks + seq_blocks),
        in_specs=in_f + in_b + [
            pl.BlockSpec((None, 2, PS_W, B_KW), lambda b, j: (layer, 0, 0, 0)),
            pl.BlockSpec((None, 2, 1, B_KW), lambda b, j: (layer, 0, 0, 0)),
        ],
        out_specs=[out_f, out_b],
        scratch_shapes=[pltpu.VMEM((2, B_VW, B_KW), F32)],
        compiler_params=_params("parallel", "arbitrary"),
        name="gla",
    )(pb, ps, pb, ps, wa_pad, ba)


def _lane_scan(x, op, identity, reverse):
    n = x.shape[1]
    lane = _iota(x.shape, 1)
    k = 1
    while k < n:
        shifted = pltpu.roll(x, n - k if reverse else k, 1)
        inside = (lane < n - k) if reverse else (lane >= k)
        x = op(x, jnp.where(inside, shifted, identity))
        k *= 2
        yield
    return x


def _round_robin(*chains):
    live = list(range(len(chains)))
    results = [None] * len(chains)
    while live:
        for i in list(live):
            try:
                next(chains[i])
            except StopIteration as stop:
                results[i] = stop.value
                live.remove(i)
    return results


def _mlstm_kernel(pcf_ref, prevf_ref, nextf_ref, psf_ref, psf_next_ref,
                  pcb_ref, prevb_ref, nextb_ref, psb_ref, psb_next_ref,
                  cw_ref, sel_ref, fb_ref, of_ref, ob_ref, c_ref, n_ref, m_ref, rows_ref, stat_ref,
                  *, ctx_blocks, seq_blocks):
    j = pl.program_id(1)

    def gates(f_ps, b_ps):
        return (_mlstm_gates(f_ps, sel_ref.at[0], fb_ref.at[0], rows_ref.at[0], stat_ref.at[0], rev=False),
                _mlstm_gates(b_ps, sel_ref.at[1], fb_ref.at[1], rows_ref.at[1], stat_ref.at[1], rev=True))

    @pl.when(j == 0)
    def _():
        c_ref[...] = jnp.zeros_like(c_ref)
        n_ref[...] = jnp.zeros_like(n_ref)
        m_ref[...] = jnp.zeros_like(m_ref)
        _round_robin(*gates(psf_ref, psb_ref))

    in_ctx = j < ctx_blocks
    n_seq = jnp.where(in_ctx, ctx_blocks, seq_blocks)
    jj = jnp.where(in_ctx, j, j - ctx_blocks)
    qkv_f, qkv_b = _round_robin(
        _mlstm_conv(pcf_ref, prevf_ref, nextf_ref, cw_ref, pos=jj, n_seq=n_seq),
        _mlstm_conv(pcb_ref, prevb_ref, nextb_ref, cw_ref, pos=n_seq - 1 - jj, n_seq=n_seq))
    _round_robin(
        _mlstm_step(qkv_f, rows_ref.at[0], stat_ref.at[0], of_ref, c_ref.at[0], n_ref.at[0], m_ref.at[0],
                    rev=False),
        _mlstm_step(qkv_b, rows_ref.at[1], stat_ref.at[1], ob_ref, c_ref.at[1], n_ref.at[1], m_ref.at[1],
                    rev=True),
        *gates(psf_next_ref, psb_next_ref))


def _mlstm_conv(pc_ref, prev_ref, next_ref, cw_ref, *, pos, n_seq):
    blk = SEQ_BLOCK
    half = blk // 2
    prev_row = jnp.where(pos > 0, prev_ref[SUBLANES - 1:SUBLANES, :], 0.0)
    next_row = jnp.where(pos < n_seq - 1, next_ref[0:1, :], 0.0)
    row = _iota((half, LANES), 0)
    tiles = []
    for c0 in range(0, 2 * C_W, LANES):
        lanes = slice(c0, c0 + LANES)
        halves = []
        for r0 in (0, half):
            x = pc_ref[r0:r0 + half, lanes]
            above = prev_row[:, lanes] if r0 == 0 else pc_ref[r0 - 1:r0, lanes]
            below = next_row[:, lanes] if r0 + half == blk else pc_ref[r0 + half:r0 + half + 1, lanes]
            x_prev = jnp.where(row == 0, above, pltpu.roll(x, 1, 0))
            x_next = jnp.where(row == half - 1, below, pltpu.roll(x, half - 1, 0))
            halves.append(_silu(x_prev * cw_ref[0:1, lanes] + x * cw_ref[1:2, lanes]
                                + x_next * cw_ref[2:3, lanes]))
            yield
        tiles.append(jnp.concatenate(halves, axis=0))
    q = jnp.concatenate(tiles[:C_W // LANES], axis=1)
    k = jnp.concatenate(tiles[C_W // LANES:], axis=1) * (C_DH ** -0.5)
    v = pc_ref[:, 2 * C_W:3 * C_W]
    head_of_lane = _div(_iota((blk, C_W), 1), C_DH)
    qk = []
    for h in range(C_HEADS):
        qk.append(_dot_nt(q, jnp.where(head_of_lane == h, k, 0.0)))
        yield
    return q, k, v, qk


def _mlstm_gates(ps_ref, sel_ref, fb_ref, rows_ref, stat_ref, *, rev):
    sub, blk, nh = SUBLANES, SEQ_BLOCK, C_HEADS
    g = _sel_dot_nt(sel_ref[...], ps_ref[...])
    li = g[0:sub]
    lf = jnp.where(_iota((sub, blk), 0) < nh, _log_sigmoid(g[sub:2 * sub] + fb_ref[:, 0:1]), 0.0)
    yield
    src_tok = _iota((blk, blk), 0)
    dst_tok = _iota((blk, blk), 1)
    upto = (src_tok >= dst_tok) if rev else (src_tok <= dst_tok)
    bcum = _dot_sel_rhs(lf, upto.astype(F32))
    yield
    btot = jnp.sum(lf, axis=1, keepdims=True)
    u = li - bcum
    u_max = yield from _lane_scan(u, jnp.maximum, -jnp.inf, rev)
    g_end = btot - bcum + li
    m_loc = jnp.max(g_end, axis=1, keepdims=True)
    rows_ref[...] = jnp.concatenate([u, u_max, bcum, jnp.exp(g_end - m_loc)], axis=0)
    stat_ref[...] = jnp.concatenate([jnp.broadcast_to(btot, (sub, LANES)),
                                     jnp.broadcast_to(m_loc, (sub, LANES))], axis=0)


def _mlstm_step(qkv, rows_ref, stat_ref, o_ref, c_ref, n_ref, m_ref, *, rev):
    blk = SEQ_BLOCK
    nh, dh = C_HEADS, C_DH
    sub = SUBLANES
    q, k, v, qk = qkv
    u = rows_ref[0:sub, :]
    u_max = rows_ref[sub:2 * sub, :]
    bcum = rows_ref[2 * sub:3 * sub, :]
    w_end = rows_ref[3 * sub:4 * sub, :]
    btot = stat_ref[0:sub, 0:1]
    m_loc = stat_ref[sub:2 * sub, 0:1]
    yield

    m_prev = m_ref[:, 0:1]
    big_m = jnp.maximum(u_max, m_prev)
    a_t = jnp.exp(m_prev - big_m)
    neg_m_t = -(bcum + big_m)
    m_new = jnp.maximum(btot + m_prev, m_loc)
    a_st = jnp.exp(btot + m_prev - m_new)
    e_st = jnp.exp(m_loc - m_new)

    pad = jnp.zeros((LANES - 4 * sub, blk), F32)
    yield
    cols = jnp.concatenate([big_m, a_t, neg_m_t, w_end, pad], axis=0).T
    yield
    width = 3 * C_W
    src_row = _iota((LANES, width), 0)
    lane = _iota((LANES, width), 1)
    src = sub * (1 + _div(lane, C_W)) + _div(lane & (C_W - 1), dh)
    wide = _dot_sel_rhs(cols, (src_row == src).astype(F32), parts=2)
    a_all = wide[:, 0:C_W]
    floor_all = jnp.exp(wide[:, C_W:2 * C_W])
    w_all = wide[:, 2 * C_W:]

    c_t = c_ref[...]
    n_row = n_ref[0:1, :]
    rr = _iota((blk, blk), 0)
    cc = _iota((blk, blk), 1)
    causal = (rr <= cc) if rev else (cc <= rr)
    head_of_lane = _div(_iota((blk, C_W), 1), dh)
    num = a_all * _dot(q, c_t)
    row_sum = None
    for h in range(nh):
        in_head = head_of_lane == h
        e = jnp.where(causal, u[h:h + 1, :] - cols[:, h:h + 1], -jnp.inf)
        sc = (qk[h] * jnp.exp(e)).astype(BF16)
        num = num + jnp.dot(sc, jnp.where(in_head, v, 0.0).astype(BF16), preferred_element_type=F32)
        rs = jnp.dot(sc, in_head.astype(BF16), preferred_element_type=F32)
        row_sum = rs if row_sum is None else row_sum + rs
        yield
    same_head = _div(_iota((C_W, C_W), 0), dh) == _div(_iota((C_W, C_W), 1), dh)
    den = a_all * _dot(q * n_row, same_head) + row_sum
    o_ref[...] = num / jnp.maximum(jnp.abs(den), floor_all)
    yield

    head8 = _div(_iota((sub, C_W), 1), dh) == _iota((sub, C_W), 0)
    a_lane = jnp.sum(jnp.where(head8, a_st, 0.0), axis=0, keepdims=True)
    e_lane = jnp.sum(jnp.where(head8, e_st, 0.0), axis=0, keepdims=True)
    c_loc = _dot(k.T, v * w_all)
    c_ref[...] = a_lane * c_t + jnp.where(same_head, e_lane * c_loc, 0.0)
    n_ref[0:1, :] = a_lane * n_row + e_lane * jnp.sum(k * w_all, axis=0, keepdims=True)
    m_ref[...] = jnp.broadcast_to(m_new, m_ref.shape)


def _mlstm(pc, ps, conv_w, gate_sel, f_bias_rows, batch, seq, ctx_len, layer):
    n_tok = pc.shape[0]
    blk = SEQ_BLOCK
    ctx_blocks, seq_blocks = ctx_len // blk, seq // blk
    rows8 = blk // SUBLANES
    last8 = n_tok // SUBLANES - 1

    def chain_specs(reverse):
        def blk_of(b, j):
            return _seq_block(b, j, reverse, batch, ctx_blocks, seq_blocks)
        return [
            pl.BlockSpec((blk, 3 * C_W), lambda b, j: (blk_of(b, j), 0)),
            pl.BlockSpec((SUBLANES, 2 * C_W), lambda b, j: (jnp.maximum(blk_of(b, j) * rows8 - 1, 0), 0)),
            pl.BlockSpec((SUBLANES, 2 * C_W), lambda b, j: (jnp.minimum((blk_of(b, j) + 1) * rows8, last8), 0)),
            pl.BlockSpec((blk, PS_W), lambda b, j: (blk_of(b, j), 0)),
            pl.BlockSpec((blk, PS_W), lambda b, j: (blk_of(b, jnp.minimum(j + 1, last_step)), 0)),
        ], pl.BlockSpec((blk, C_W), lambda b, j: (blk_of(b, j), 0))

    last_step = ctx_blocks + seq_blocks - 1
    in_f, out_f = chain_specs(False)
    in_b, out_b = chain_specs(True)
    out = jax.ShapeDtypeStruct((n_tok, C_W), F32)
    return pl.pallas_call(
        functools.partial(_mlstm_kernel, ctx_blocks=ctx_blocks, seq_blocks=seq_blocks),
        out_shape=[out, out],
        grid=(batch, ctx_blocks + seq_blocks),
        in_specs=in_f + in_b + [
            pl.BlockSpec((None, 3, 2 * C_W), lambda b, j: (layer, 0, 0)),
            pl.BlockSpec((2, 2 * SUBLANES, PS_W), lambda b, j: (0, 0, 0)),
            pl.BlockSpec((None, 2, SUBLANES, LANES), lambda b, j: (layer, 0, 0, 0)),
        ],
        out_specs=[out_f, out_b],
        scratch_shapes=[pltpu.VMEM((2, C_W, C_W), F32), pltpu.VMEM((2, SUBLANES, C_W), F32),
                        pltpu.VMEM((2, SUBLANES, LANES), F32), pltpu.VMEM((2, 4 * SUBLANES, blk), F32),
                        pltpu.VMEM((2, 2 * SUBLANES, LANES), F32)],
        compiler_params=_params("parallel", "arbitrary"),
        name="mlstm",
    )(pc, pc, pc, ps, ps, pc, pc, pc, ps, ps, conv_w, gate_sel, f_bias_rows)


def _head_layernorm(o, g, width):
    n = o.shape[1]
    avg = jnp.where(_div(_iota((n, n), 0), width) == _div(_iota((n, n), 1), width), 1.0 / width, 0.0)
    mu = _dot_sel_rhs(o, avg, parts=2)
    cen = o - mu
    var = _dot_sel_rhs(cen * cen, avg, parts=2)
    return cen * lax.rsqrt(var + EPS) * g


def _out_kernel(x_ref, mod_ref, g_ref, ya_ref, gb_ref, obf_ref, obb_ref, gc_ref, ocf_ref, ocb_ref,
                gng_ref, mng_ref, w_ref, o_ref):
    y_b = _silu(gb_ref[...]) * _head_layernorm(obf_ref[...] + obb_ref[...], gng_ref[...], B_DV)
    y_c = _sigmoid(gc_ref[...]) * _head_layernorm(ocf_ref[...] + ocb_ref[...], mng_ref[...], C_DH)
    y = (_dot(ya_ref[...], w_ref[0:A_OUT, :])
         + _dot(y_b, w_ref[A_OUT:A_OUT + B_VW, :])
         + _dot(y_c, w_ref[A_OUT + B_VW:, :]))
    o_ref[...] = x_ref[...] + mod_ref[5:6, :] * _rms(y, g_ref[3:4, :])


def _out(tok, modtab, norm_g, ya, pb, ob_f, ob_b, pc, oc_f, oc_b, gla_norm, mlstm_norm, w_out, geom, layer):
    tm = geom.tm
    n_tok, d = tok.shape
    gb_col = (2 * B_KW + B_VW) // B_VW
    gc_col = 3
    return pl.pallas_call(
        _out_kernel,
        out_shape=jax.ShapeDtypeStruct((n_tok, d), F32),
        grid=(geom.tiles,),
        in_specs=[
            pl.BlockSpec((tm, d), lambda i: (i, 0)),
            pl.BlockSpec((None, None, N_MOD, d), lambda i: (layer, geom.mod_row(i), 0, 0)),
            pl.BlockSpec((None, 6, d), lambda i: (layer, 0, 0)),
            pl.BlockSpec((tm, A_OUT), lambda i: (i, 0)),
            pl.BlockSpec((tm, B_VW), lambda i: (i, gb_col)),
            pl.BlockSpec((tm, B_VW), lambda i: (i, 0)),
            pl.BlockSpec((tm, B_VW), lambda i: (i, 0)),
            pl.BlockSpec((tm, C_W), lambda i: (i, gc_col)),
            pl.BlockSpec((tm, C_W), lambda i: (i, 0)),
            pl.BlockSpec((tm, C_W), lambda i: (i, 0)),
            pl.BlockSpec((None, 1, B_VW), lambda i: (layer, 0, 0)),
            pl.BlockSpec((None, 1, C_W), lambda i: (layer, 0, 0)),
            pl.BlockSpec((None, d, d), lambda i: (layer, 0, 0)),
        ],
        out_specs=pl.BlockSpec((tm, d), lambda i: (i, 0)),
        compiler_params=_params("parallel"),
        name="out",
    )(tok, modtab, norm_g, ya, pb, ob_f, ob_b, pc, oc_f, oc_b, gla_norm, mlstm_norm, w_out)


def _rope_tables(seq, tm):
    pos = jnp.arange(seq)
    half = ROPE_AXIS // 2
    inv_freq = ROPE_BASE ** (-jnp.arange(half, dtype=F32) / half)
    ang_r = (pos // GRID_W).astype(F32)[:, None] * inv_freq[None, :]
    ang_c = (pos % GRID_W).astype(F32)[:, None] * inv_freq[None, :]
    cos = jnp.concatenate([jnp.cos(ang_r)] * 2 + [jnp.cos(ang_c)] * 2, axis=1)
    sin = jnp.concatenate([-jnp.sin(ang_r), jnp.sin(ang_r), -jnp.sin(ang_c), jnp.sin(ang_c)], axis=1)
    reps = LANES // A_HEAD_DIM
    cos = jnp.concatenate([jnp.tile(cos, (1, reps)), jnp.ones((tm, LANES), F32)], axis=0)
    sin = jnp.concatenate([jnp.tile(sin, (1, reps)), jnp.zeros((tm, LANES), F32)], axis=0)
    return cos, sin


def _regroup_in(w):
    a_end = PA_W + PB_W
    alpha = slice(a_end, a_end + 2 * GLA_RANK)
    c_qkv = slice(alpha.stop, alpha.stop + 3 * C_W)
    c_gate = slice(c_qkv.stop, c_qkv.stop + 4 * C_HEADS)
    c_og = slice(c_gate.stop, c_gate.stop + C_W)
    pad = jnp.zeros(w.shape[:-1] + (PS_W - 2 * GLA_RANK - 4 * C_HEADS,), w.dtype)
    return jnp.concatenate([w[..., :a_end], w[..., c_qkv], w[..., c_og], w[..., alpha], w[..., c_gate], pad],
                           axis=-1)


def _pick_tile(n_ctx, seq, largest):
    tm = largest
    while tm >= LANES:
        if n_ctx % tm == 0 and seq % tm == 0:
            return tm
        tm //= 2
    raise ValueError("token counts must be multiples of 128")


def kernel(x, c, ctx, c_ctx, ada_w, ada_b, norm_g, ffn1_wg, ffn1_wu, ffn1_wd, ffn2_wg, ffn2_wu, ffn2_wd,
           w_in, b_in, w_out, attn_sink, gla_w_alpha, gla_b_alpha, gla_norm, mlstm_conv, mlstm_f_bias,
           mlstm_norm):
    batch, seq, d = x.shape
    ctx_len = ctx.shape[1]
    n_l = ada_w.shape[0]
    assert seq % SEQ_BLOCK == 0 and ctx_len % SEQ_BLOCK == 0
    geom = _Geom(batch, seq, ctx_len, _pick_tile(batch * ctx_len, seq, TOKEN_TILE))
    geom_ffn = _Geom(batch, seq, ctx_len, _pick_tile(batch * ctx_len, seq, FFN_TOKEN_TILE))

    rows = -(-(batch + 1) // SUBLANES) * SUBLANES
    cc = jnp.concatenate([c, c_ctx[None, :], jnp.zeros((rows - batch - 1, d), F32)], axis=0)
    modtab = _ada_table(cc, ada_w, ada_b)

    rope_cos, rope_sin = _rope_tables(seq, geom.tm)
    w_in_r = _regroup_in(w_in.astype(BF16))
    b_in_r = _regroup_in(b_in)[:, None, :]
    wg1, wu1, wd1 = ffn1_wg.astype(BF16), ffn1_wu.astype(BF16), ffn1_wd.astype(BF16)
    wg2, wu2, wd2 = ffn2_wg.astype(BF16), ffn2_wu.astype(BF16), ffn2_wd.astype(BF16)
    w_out_b = w_out.astype(BF16)

    wa_pad = jnp.zeros((n_l, 2, PS_W, B_KW), F32)
    for dirn in range(2):
        wa_pad = wa_pad.at[:, dirn, dirn * GLA_RANK:(dirn + 1) * GLA_RANK, :].set(gla_w_alpha[:, dirn])
    wa_pad = wa_pad.astype(BF16)
    ba = gla_b_alpha[:, :, None, :]

    nh = C_HEADS
    out_idx = jnp.arange(2 * SUBLANES)[:, None]
    src_idx = jnp.arange(PS_W)[None, :]
    head_idx = out_idx % SUBLANES
    gate_sel = jnp.stack([
        (head_idx < nh) & (src_idx == PS_GATE0 + 2 * nh * dirn + nh * (out_idx // SUBLANES) + head_idx)
        for dirn in range(2)]).astype(F32)
    f_bias_rows = jnp.zeros((n_l, 2, SUBLANES), F32).at[:, :, 0:nh].set(mlstm_f_bias)
    f_bias_rows = jnp.broadcast_to(f_bias_rows[:, :, :, None], (n_l, 2, SUBLANES, LANES))

    sink_rows = jnp.broadcast_to(attn_sink[:, :, None], (n_l, A_HEADS, LANES))
    gla_g = gla_norm[:, None, :]
    mlstm_g = mlstm_norm[:, None, :]

    tok = jnp.concatenate([ctx.reshape(batch * ctx_len, d), x.reshape(batch * seq, d)], axis=0)
    for layer in range(n_l):
        last = layer == n_l - 1
        tok = _ffn(tok, modtab, norm_g, wg1, wu1, wd1, geom_ffn, layer, 0, 0)
        pa, pb, pc, ps = _inproj(tok, modtab, norm_g, w_in_r, b_in_r, rope_cos, rope_sin, geom, layer)
        ya = _attn(pa, sink_rows[layer], batch, seq, ctx_len)
        ob_f, ob_b = _gla(pb, ps, wa_pad, ba, batch, seq, ctx_len, layer)
        oc_f, oc_b = _mlstm(pc, ps, mlstm_conv, gate_sel, f_bias_rows, batch, seq, ctx_len, layer)
        tok = _out(tok, modtab, norm_g, ya, pb, ob_f, ob_b, pc, oc_f, oc_b, gla_g, mlstm_g, w_out_b, geom, layer)
        tok = _ffn(tok, modtab, norm_g, wg2, wu2, wd2, geom_ffn, layer, 6, 4, latent_only=last)
    return tok.reshape(batch, seq, d)
```

```python
import functools

import jax
import jax.numpy as jnp
from jax import lax
from jax.experimental import pallas as pl
from jax.experimental.pallas import tpu as pltpu

F32 = jnp.float32
BF16 = jnp.bfloat16

GRID_W = 64
A_HEAD_DIM = 64
A_HEADS = 8
A_KV_HEADS = 2
A_REP = A_HEADS // A_KV_HEADS
WINDOW = 128
ROPE_BASE = 10000.0
ROPE_AXIS = A_HEAD_DIM // 2
B_DK = 32
B_DV = 64
B_HEADS = 4
GLA_RANK = 16
GLA_TAU = 16.0
GLA_CHUNK = 16
C_DH = 64
C_HEADS = 4
A_OUT = A_HEADS * A_HEAD_DIM
A_KV = A_KV_HEADS * A_HEAD_DIM
B_KW = B_HEADS * B_DK
B_VW = B_HEADS * B_DV
C_W = C_HEADS * C_DH
FFN_RES = 0.5
N_MOD = 9
EPS = 1e-6
LOG2_E = 1.4426950408889634

PA_W = A_OUT + 2 * A_KV
PB_W = 2 * B_KW + 2 * B_VW
PC_W = 4 * C_W
PS_W = 128
PS_GATE0 = 2 * GLA_RANK

LANES = 128
SUBLANES = 8
SEQ_BLOCK = 256
ATT_BLOCK = WINDOW
ATT_QUERIES = 2 * ATT_BLOCK
FF_CHUNK = 256
TOKEN_TILE = 512
FFN_TOKEN_TILE = 1024
V7X_VMEM_BYTES = 64 * 1024 * 1024
VMEM_LIMIT = V7X_VMEM_BYTES * 7 // 8


def _sigmoid(x):
    return 1.0 / (1.0 + jnp.exp(-x))


def _silu(x):
    return x * _sigmoid(x)


def _log_sigmoid(x):
    return jnp.minimum(x, 0.0) - jnp.log(1.0 + jnp.exp(-jnp.abs(x)))


def _rms(x, g):
    return x * lax.rsqrt(jnp.mean(x * x, axis=-1, keepdims=True) + EPS) * g


def _dot(a, b):
    return jnp.dot(a.astype(BF16), b.astype(BF16), preferred_element_type=F32)


def _dot_nt(a, b):
    return lax.dot_general(a.astype(BF16), b.astype(BF16), (((1,), (1,)), ((), ())),
                           preferred_element_type=F32)


def _split(x, parts):
    out = []
    for _ in range(parts):
        t = x.astype(BF16)
        out.append(t)
        x = x - t.astype(F32)
    return out


def _dot_sel(sel, x, parts=3):
    sel = sel.astype(BF16)
    acc = None
    for t in _split(x, parts):
        y = jnp.dot(sel, t, preferred_element_type=F32)
        acc = y if acc is None else acc + y
    return acc


def _dot_sel_rhs(x, sel, parts=3):
    sel = sel.astype(BF16)
    acc = None
    for t in _split(x, parts):
        y = jnp.dot(t, sel, preferred_element_type=F32)
        acc = y if acc is None else acc + y
    return acc


def _sel_dot_nt(sel, x, parts=3):
    sel = sel.astype(BF16)
    acc = None
    for t in _split(x, parts):
        y = lax.dot_general(sel, t, (((1,), (1,)), ((), ())), preferred_element_type=F32)
        acc = y if acc is None else acc + y
    return acc


def _iota(shape, dim):
    return lax.broadcasted_iota(jnp.int32, shape, dim)


def _div(x, n):
    assert n & (n - 1) == 0
    return x >> (n.bit_length() - 1)


def _params(*semantics):
    return pltpu.CompilerParams(dimension_semantics=semantics, vmem_limit_bytes=VMEM_LIMIT)


def _ada_kernel(c_ref, w_ref, b_ref, o_ref):
    o_ref[...] = _dot(_silu(c_ref[...]), w_ref[...]) + b_ref[...]


def _ada_table(cc, ada_w, ada_b):
    n_l, d, _ = ada_w.shape
    r = cc.shape[0]
    out = pl.pallas_call(
        _ada_kernel,
        out_shape=jax.ShapeDtypeStruct((n_l, N_MOD, r, d), F32),
        grid=(n_l, N_MOD),
        in_specs=[
            pl.BlockSpec((r, d), lambda l, j: (0, 0)),
            pl.BlockSpec((None, d, d), lambda l, j: (l, 0, j)),
            pl.BlockSpec((None, 1, d), lambda l, j: (l, 0, j)),
        ],
        out_specs=pl.BlockSpec((None, None, r, d), lambda l, j: (l, j, 0, 0)),
        compiler_params=_params("parallel", "parallel"),
        name="ada",
    )(cc, ada_w, ada_b.reshape(n_l, 1, N_MOD * d))
    return out.transpose(0, 2, 1, 3)


class _Geom:
    def __init__(self, batch, seq, ctx_len, tm):
        self.batch, self.seq, self.ctx_len, self.tm = batch, seq, ctx_len, tm
        self.n_ctx = batch * ctx_len
        self.n_tok = batch * (ctx_len + seq)
        assert self.n_ctx % tm == 0 and seq % tm == 0
        self.ctx_tiles = self.n_ctx // tm
        self.tiles_per_batch = seq // tm
        self.tiles = self.n_tok // tm

    def mod_row(self, i):
        return jnp.where(i < self.ctx_tiles, self.batch, (i - self.ctx_tiles) // self.tiles_per_batch)

    def rope_block(self, i):
        return jnp.where(i < self.ctx_tiles, self.tiles_per_batch,
                         (i - self.ctx_tiles) % self.tiles_per_batch)


def _ffn_kernel(x_ref, mod_ref, g_ref, wg_ref, wu_ref, wd_ref, o_ref, *, mod_base, g_base):
    x = x_ref[...]
    mod = mod_ref[...]
    g = g_ref[...]
    shift = mod[mod_base:mod_base + 1]
    scale = mod[mod_base + 1:mod_base + 2]
    gate = mod[mod_base + 2:mod_base + 3]
    u = (_rms(x, g[g_base:g_base + 1]) * (1.0 + scale) + shift).astype(BF16)
    d_ff = wg_ref.shape[1]
    y = None
    for c0 in range(0, d_ff, FF_CHUNK):
        a = jnp.dot(u, wg_ref[:, c0:c0 + FF_CHUNK], preferred_element_type=F32)
        b = jnp.dot(u, wu_ref[:, c0:c0 + FF_CHUNK], preferred_element_type=F32)
        hcol = (_silu(a) * b).astype(BF16)
        part = jnp.dot(hcol, wd_ref[c0:c0 + FF_CHUNK, :], preferred_element_type=F32)
        y = part if y is None else y + part
    o_ref[...] = x + FFN_RES * gate * _rms(y, g[g_base + 1:g_base + 2])


def _ffn(tok, modtab, norm_g, wg, wu, wd, geom, layer, mod_base, g_base, latent_only=False):
    tm = geom.tm
    d = tok.shape[1]
    d_ff = wg.shape[2]
    assert d_ff % FF_CHUNK == 0
    first = geom.ctx_tiles if latent_only else 0
    n_tiles = geom.tiles - first
    return pl.pallas_call(
        functools.partial(_ffn_kernel, mod_base=mod_base, g_base=g_base),
        out_shape=jax.ShapeDtypeStruct((n_tiles * tm, d), F32),
        grid=(n_tiles,),
        in_specs=[
            pl.BlockSpec((tm, d), lambda i: (i + first, 0)),
            pl.BlockSpec((None, None, N_MOD, d), lambda i: (layer, geom.mod_row(i + first), 0, 0)),
            pl.BlockSpec((None, 6, d), lambda i: (layer, 0, 0)),
            pl.BlockSpec((None, d, d_ff), lambda i: (layer, 0, 0), pipeline_mode=pl.Buffered(1)),
            pl.BlockSpec((None, d, d_ff), lambda i: (layer, 0, 0), pipeline_mode=pl.Buffered(1)),
            pl.BlockSpec((None, d_ff, d), lambda i: (layer, 0, 0), pipeline_mode=pl.Buffered(1)),
        ],
        out_specs=pl.BlockSpec((tm, d), lambda i: (i, 0)),
        compiler_params=_params("parallel"),
        name="ffn",
    )(tok, modtab, norm_g, wg, wu, wd)


def _inproj_kernel(x_ref, mod_ref, g_ref, w_ref, b_ref, cos_ref, sin_ref,
                   pa_ref, pb_ref, pc_ref, ps_ref):
    x = x_ref[...]
    mod = mod_ref[...]
    g = g_ref[...]
    u = _rms(x, g[2:3]) * (1.0 + mod[4:5]) + mod[3:4]
    p = _dot(u, w_ref[...]) + b_ref[...]
    qk_w = A_OUT + A_KV
    qk = p[:, 0:qk_w]
    reps = qk_w // LANES
    cos = jnp.concatenate([cos_ref[...]] * reps, axis=1)
    sin = jnp.concatenate([sin_ref[...]] * reps, axis=1)
    half = ROPE_AXIS // 2
    first_half = (_iota(qk.shape, 1) & (ROPE_AXIS - 1)) < half
    partner = jnp.where(first_half, pltpu.roll(qk, qk_w - half, 1), pltpu.roll(qk, half, 1))
    rot = qk * cos + partner * sin
    pa_ref[:, 0:A_OUT] = rot[:, 0:A_OUT] * (A_HEAD_DIM ** -0.5 * LOG2_E)
    pa_ref[:, A_OUT:qk_w] = rot[:, A_OUT:qk_w]
    pa_ref[:, qk_w:PA_W] = p[:, qk_w:PA_W]
    pb_ref[...] = p[:, PA_W:PA_W + PB_W]
    pc_ref[...] = p[:, PA_W + PB_W:PA_W + PB_W + PC_W]
    ps_ref[...] = p[:, PA_W + PB_W + PC_W:]


def _inproj(tok, modtab, norm_g, w_in, b_in, rope_cos, rope_sin, geom, layer):
    tm = geom.tm
    d = tok.shape[1]
    n_tok = tok.shape[0]
    width = w_in.shape[2]
    outs = [jax.ShapeDtypeStruct((n_tok, w), F32) for w in (PA_W, PB_W, PC_W, PS_W)]
    return pl.pallas_call(
        _inproj_kernel,
        out_shape=outs,
        grid=(geom.tiles,),
        in_specs=[
            pl.BlockSpec((tm, d), lambda i: (i, 0)),
            pl.BlockSpec((None, None, N_MOD, d), lambda i: (layer, geom.mod_row(i), 0, 0)),
            pl.BlockSpec((None, 6, d), lambda i: (layer, 0, 0)),
            pl.BlockSpec((None, d, width), lambda i: (layer, 0, 0)),
            pl.BlockSpec((None, 1, width), lambda i: (layer, 0, 0)),
            pl.BlockSpec((tm, LANES), lambda i: (geom.rope_block(i), 0)),
            pl.BlockSpec((tm, LANES), lambda i: (geom.rope_block(i), 0)),
        ],
        out_specs=[pl.BlockSpec((tm, w), lambda i: (i, 0)) for w in (PA_W, PB_W, PC_W, PS_W)],
        compiler_params=_params("parallel"),
        name="inproj",
    )(tok, modtab, norm_g, w_in, b_in, rope_cos, rope_sin)


def _attn_kernel(q_ref, k0_ref, k1_ref, k2_ref, k3_ref, v0_ref, v1_ref, v2_ref, v3_ref, kc_ref, vc_ref,
                 sink_ref, o_ref, *, ctx_steps, seq_len):
    n = pl.program_id(1) - ctx_steps
    qb = ATT_QUERIES
    n_win = 4 * ATT_BLOCK
    hd = A_HEAD_DIM
    k_abs = n * qb - WINDOW + _iota((n_win, qb), 0)
    q_abs = n * qb + _iota((n_win, qb), 1)
    hi = jnp.where(n >= 0, seq_len, 0)
    mask = (jnp.abs(q_abs - k_abs) <= WINDOW) & (k_abs >= 0) & (k_abs < hi)
    q_t = q_ref[...].T
    k_all = jnp.concatenate([k0_ref[...], k1_ref[...], k2_ref[...], k3_ref[...], kc_ref[...]], axis=0)
    v_t = jnp.concatenate([v0_ref[...], v1_ref[...], v2_ref[...], v3_ref[...], vc_ref[...]], axis=0).T
    zero = jnp.zeros((hd, qb), F32)
    cols, sinks = [], []
    for h in range(A_HEADS):
        q_h = q_t[h * hd:(h + 1) * hd]
        cols.append(jnp.concatenate([q_h, zero] if h // A_REP == 0 else [zero, q_h], axis=0))
        sinks.append(jnp.broadcast_to(sink_ref[h:h + 1, 0:1] * LOG2_E, (1, qb)))
    sk = jnp.concatenate(sinks, axis=1)
    s = _dot(k_all, jnp.concatenate(cols, axis=1))
    s_win = jnp.where(jnp.concatenate([mask] * A_HEADS, axis=1), s[:n_win], -jnp.inf)
    s_ctx = s[n_win:]
    m = jnp.maximum(jnp.maximum(jnp.max(s_win, axis=0, keepdims=True),
                                jnp.max(s_ctx, axis=0, keepdims=True)), sk)
    p = jnp.concatenate([jnp.exp2((s_win - m).astype(BF16)), jnp.exp2((s_ctx - m).astype(BF16))], axis=0)
    ones = jnp.ones((2 * SUBLANES, v_t.shape[1]), BF16)
    per_group = A_REP * qb
    outs = []
    for g in range(A_KV_HEADS):
        lanes = slice(g * per_group, (g + 1) * per_group)
        v_aug = jnp.concatenate([v_t[g * hd:(g + 1) * hd].astype(BF16), ones], axis=0)
        o_aug = jnp.dot(v_aug, p[:, lanes], preferred_element_type=F32)
        den = o_aug[hd:hd + 1] + jnp.exp2(sk[:, lanes] - m[:, lanes])
        o_g = o_aug[0:hd] / den
        outs += [o_g[:, r * qb:(r + 1) * qb] for r in range(A_REP)]
    o_ref[...] = jnp.concatenate(outs, axis=0).T


def _attn(pa, sink_rows, batch, seq, ctx_len):
    n_tok = pa.shape[0]
    qb = ATT_QUERIES
    blk = ATT_BLOCK
    per_q = qb // blk
    ctx_steps = ctx_len // qb
    seq_steps = seq // qb
    seq_blocks = seq // blk
    lat0 = batch * (ctx_len // blk)
    k_col = A_OUT // LANES
    v_col = k_col + 1

    def q_map(b, j):
        return (jnp.where(j < ctx_steps, b * ctx_steps + j, batch * ctx_steps + b * seq_steps + j - ctx_steps), 0)

    def kv_map(i, col):
        def f(b, j):
            n = jnp.clip((j - ctx_steps) * per_q - 1 + i, 0, seq_blocks - 1)
            return (lat0 + b * seq_blocks + n, col)
        return f

    kv_specs = [pl.BlockSpec((blk, LANES), kv_map(i, col))
                for col in (k_col, v_col) for i in range(per_q + 2)]
    return pl.pallas_call(
        functools.partial(_attn_kernel, ctx_steps=ctx_steps, seq_len=seq),
        out_shape=jax.ShapeDtypeStruct((n_tok, A_OUT), F32),
        grid=(batch, ctx_steps + seq_steps),
        in_specs=[pl.BlockSpec((qb, A_OUT), q_map)] + kv_specs + [
            pl.BlockSpec((ctx_len, LANES), lambda b, j: (b, k_col)),
            pl.BlockSpec((ctx_len, LANES), lambda b, j: (b, v_col)),
            pl.BlockSpec((SUBLANES, LANES), lambda b, j: (0, 0)),
        ],
        out_specs=pl.BlockSpec((qb, A_OUT), q_map),
        compiler_params=_params("parallel", "parallel"),
        name="attn",
    )(*([pa] * (2 * (per_q + 2) + 3)), sink_rows)


def _seq_block(b, j, reverse, batch, ctx_blocks, seq_blocks):
    jc = jnp.where(reverse, ctx_blocks - 1 - j, j)
    jl = jnp.where(reverse, seq_blocks - 1 - (j - ctx_blocks), j - ctx_blocks)
    return jnp.where(j < ctx_blocks, b * ctx_blocks + jc, batch * ctx_blocks + b * seq_blocks + jl)


def _gla_kernel(pbf_ref, psf_ref, pbf_next_ref, psf_next_ref, pbb_ref, psb_ref, pbb_next_ref, psb_next_ref,
                wa_ref, ba_ref, of_ref, ob_ref, st_ref, prep_ref, vt_ref):
    def preps(f_pb, f_ps, b_pb, b_ps):
        return (_gla_prep(f_pb, f_ps, wa_ref.at[0], ba_ref.at[0], prep_ref.at[0], vt_ref.at[0], reverse=False),
                _gla_prep(b_pb, b_ps, wa_ref.at[1], ba_ref.at[1], prep_ref.at[1], vt_ref.at[1], reverse=True))

    @pl.when(pl.program_id(1) == 0)
    def _():
        st_ref[...] = jnp.zeros_like(st_ref)
        _round_robin(*preps(pbf_ref, psf_ref, pbb_ref, psb_ref))

    _round_robin(_gla_chain(pbf_ref, prep_ref.at[0], vt_ref.at[0], of_ref, st_ref.at[0], reverse=False),
                 _gla_chain(pbb_ref, prep_ref.at[1], vt_ref.at[1], ob_ref, st_ref.at[1], reverse=True),
                 *preps(pbf_next_ref, psf_next_ref, pbb_next_ref, psb_next_ref))


def _gla_prep(pb_ref, ps_ref, wa_ref, ba_ref, prep_ref, vt_ref, *, reverse):
    blk = SEQ_BLOCK
    shift = GLA_CHUNK.bit_length() - 1
    log_a = _log_sigmoid(_dot(ps_ref[...], wa_ref[...]) + ba_ref[...]) * (1.0 / GLA_TAU)
    yield
    r = _iota((blk, blk), 0)
    c = _iota((blk, blk), 1)
    same = (r >> shift) == (c >> shift)
    tri = same & ((c >= r) if reverse else (c <= r))
    bcum = _dot_sel(tri.astype(F32), log_a)
    yield
    btot = _dot_sel(same.astype(F32), log_a)
    yield
    q_in = pb_ref[:, 0:B_KW] * (B_DK ** -0.5) * jnp.exp(bcum)
    k_out = pb_ref[:, B_KW:2 * B_KW] * jnp.exp(btot - bcum)
    yield
    v_t = pb_ref[:, 2 * B_KW:2 * B_KW + B_VW].T
    yield
    prep_ref[0] = bcum
    prep_ref[1] = btot
    prep_ref[2] = q_in
    prep_ref[3] = k_out
    vt_ref[...] = v_t


def _gla_chain(pb_ref, prep_ref, vt_ref, o_ref, st_ref, *, reverse):
    blk = SEQ_BLOCK
    ch = GLA_CHUNK
    half = ch // 2
    assert half == SUBLANES
    n_ch = blk // ch
    shift = ch.bit_length() - 1
    q = pb_ref[:, 0:B_KW] * (B_DK ** -0.5)
    k = pb_ref[:, B_KW:2 * B_KW]
    v = pb_ref[:, 2 * B_KW:2 * B_KW + B_VW]
    bcum, btot, q_in, k_out = prep_ref[0], prep_ref[1], prep_ref[2], prep_ref[3]
    v_t = vt_ref[...]

    row_chunk = _iota((blk, B_KW), 0) >> shift
    t_full = _iota((ch, B_KW), 0)
    t_half = _iota((half, B_KW), 0)
    head_expand = _div(_iota((B_KW, B_VW), 0), B_DK) == _div(_iota((B_KW, B_VW), 1), B_DV)
    head_expand = head_expand.astype(BF16)
    state_mask = _div(_iota((B_VW, B_KW), 0), B_DV) == _div(_iota((B_VW, B_KW), 1), B_DK)
    full_src = range(half, ch) if reverse else range(half)
    half_src = range(half) if reverse else range(half, ch)
    h0 = 0 if reverse else half
    yield

    st = st_ref[...]
    outs = [None] * n_ch
    for ci in (range(n_ch - 1, -1, -1) if reverse else range(n_ch)):
        lo = ci * ch
        q_c, k_c, b_c, v_c = q[lo:lo + ch], k[lo:lo + ch], bcum[lo:lo + ch], v[lo:lo + ch]
        q_h, b_h = q_c[h0:h0 + half], b_c[h0:h0 + half]
        pieces = []
        for s in full_src:
            seen = (t_full <= s) if reverse else (t_full >= s)
            arg = jnp.where(seen, b_c - b_c[s:s + 1], -jnp.inf)
            pieces.append(q_c * k_c[s:s + 1] * jnp.exp(arg))
        for s in half_src:
            seen = (t_half <= s - h0) if reverse else (t_half >= s - h0)
            arg = jnp.where(seen, b_h - b_c[s:s + 1], -jnp.inf)
            pieces.append(q_h * k_c[s:s + 1] * jnp.exp(arg))
        w = _dot(jnp.concatenate(pieces, axis=0), head_expand)
        o_c = _dot_nt(q_in[lo:lo + ch], st)
        o_halves = [o_c[0:half], o_c[half:ch]]
        row = 0
        for s in full_src:
            for part in range(2):
                o_halves[part] = o_halves[part] + w[row:row + half] * v_c[s:s + 1]
                row += half
        for s in half_src:
            part = h0 // half
            o_halves[part] = o_halves[part] + w[row:row + half] * v_c[s:s + 1]
            row += half
        outs[ci] = jnp.concatenate(o_halves, axis=0)
        st_loc = _dot(v_t, jnp.where(row_chunk == ci, k_out, 0.0))
        st = st * jnp.exp(btot[lo:lo + 1]) + jnp.where(state_mask, st_loc, 0.0)
        yield
    st_ref[...] = st
    o_ref[...] = jnp.concatenate(outs, axis=0)


def _gla(pb, ps, wa_pad, ba, batch, seq, ctx_len, layer):
    n_tok = pb.shape[0]
    blk = SEQ_BLOCK
    ctx_blocks, seq_blocks = ctx_len // blk, seq // blk

    def chain_specs(reverse):
        def blk_map(b, j):
            return (_seq_block(b, j, reverse, batch, ctx_blocks, seq_blocks), 0)

        def next_map(b, j):
            return blk_map(b, jnp.minimum(j + 1, ctx_blocks + seq_blocks - 1))
        return [pl.BlockSpec((blk, 2 * B_KW + B_VW), blk_map), pl.BlockSpec((blk, PS_W), blk_map),
                pl.BlockSpec((blk, 2 * B_KW + B_VW), next_map), pl.BlockSpec((blk, PS_W), next_map)], \
            pl.BlockSpec((blk, B_VW), blk_map)

    in_f, out_f = chain_specs(False)
    in_b, out_b = chain_specs(True)
    out = jax.ShapeDtypeStruct((n_tok, B_VW), F32)
    return pl.pallas_call(
        _gla_kernel,
        out_shape=[out, out],
        grid=(batch, ctx_blocks + seq_blocks),
        in_specs=in_f + in_b + [
            pl.BlockSpec((None, 2, PS_W, B_KW), lambda b, j: (layer, 0, 0, 0)),
            pl.BlockSpec((None, 2, 1, B_KW), lambda b, j: (layer, 0, 0, 0)),
        ],
        out_specs=[out_f, out_b],
        scratch_shapes=[pltpu.VMEM((2, B_VW, B_KW), F32), pltpu.VMEM((2, 4, blk, B_KW), F32),
                        pltpu.VMEM((2, B_VW, blk), F32)],
        compiler_params=_params("parallel", "arbitrary"),
        name="gla",
    )(pb, ps, pb, ps, pb, ps, pb, ps, wa_pad, ba)


def _lane_scan(x, op, identity, reverse):
    n = x.shape[1]
    lane = _iota(x.shape, 1)
    k = 1
    while k < n:
        shifted = pltpu.roll(x, n - k if reverse else k, 1)
        inside = (lane < n - k) if reverse else (lane >= k)
        x = op(x, jnp.where(inside, shifted, identity))
        k *= 2
        yield
    return x


def _round_robin(*chains):
    live = list(range(len(chains)))
    results = [None] * len(chains)
    while live:
        for i in list(live):
            try:
                next(chains[i])
            except StopIteration as stop:
                results[i] = stop.value
                live.remove(i)
    return results


def _mlstm_kernel(pcf_ref, prevf_ref, nextf_ref, psf_ref, psf_next_ref,
                  pcb_ref, prevb_ref, nextb_ref, psb_ref, psb_next_ref,
                  cw_ref, sel_ref, fb_ref, of_ref, ob_ref, c_ref, n_ref, m_ref, rows_ref, stat_ref,
                  *, ctx_blocks, seq_blocks):
    j = pl.program_id(1)

    def gates(f_ps, b_ps):
        return (_mlstm_gates(f_ps, sel_ref.at[0], fb_ref.at[0], rows_ref.at[0], stat_ref.at[0], rev=False),
                _mlstm_gates(b_ps, sel_ref.at[1], fb_ref.at[1], rows_ref.at[1], stat_ref.at[1], rev=True))

    @pl.when(j == 0)
    def _():
        c_ref[...] = jnp.zeros_like(c_ref)
        n_ref[...] = jnp.zeros_like(n_ref)
        m_ref[...] = jnp.zeros_like(m_ref)
        _round_robin(*gates(psf_ref, psb_ref))

    in_ctx = j < ctx_blocks
    n_seq = jnp.where(in_ctx, ctx_blocks, seq_blocks)
    jj = jnp.where(in_ctx, j, j - ctx_blocks)
    qkv_f, qkv_b = _round_robin(
        _mlstm_conv(pcf_ref, prevf_ref, nextf_ref, cw_ref, pos=jj, n_seq=n_seq),
        _mlstm_conv(pcb_ref, prevb_ref, nextb_ref, cw_ref, pos=n_seq - 1 - jj, n_seq=n_seq))
    _round_robin(
        _mlstm_step(qkv_f, rows_ref.at[0], stat_ref.at[0], of_ref, c_ref.at[0], n_ref.at[0], m_ref.at[0],
                    rev=False),
        _mlstm_step(qkv_b, rows_ref.at[1], stat_ref.at[1], ob_ref, c_ref.at[1], n_ref.at[1], m_ref.at[1],
                    rev=True),
        *gates(psf_next_ref, psb_next_ref))


def _mlstm_conv(pc_ref, prev_ref, next_ref, cw_ref, *, pos, n_seq):
    blk = SEQ_BLOCK
    half = blk // 2
    prev_row = jnp.where(pos > 0, prev_ref[SUBLANES - 1:SUBLANES, :], 0.0)
    next_row = jnp.where(pos < n_seq - 1, next_ref[0:1, :], 0.0)
    row = _iota((half, LANES), 0)
    tiles = []
    for c0 in range(0, 2 * C_W, LANES):
        lanes = slice(c0, c0 + LANES)
        halves = []
        for r0 in (0, half):
            x = pc_ref[r0:r0 + half, lanes]
            above = prev_row[:, lanes] if r0 == 0 else pc_ref[r0 - 1:r0, lanes]
            below = next_row[:, lanes] if r0 + half == blk else pc_ref[r0 + half:r0 + half + 1, lanes]
            x_prev = jnp.where(row == 0, above, pltpu.roll(x, 1, 0))
            x_next = jnp.where(row == half - 1, below, pltpu.roll(x, half - 1, 0))
            halves.append(_silu(x_prev * cw_ref[0:1, lanes] + x * cw_ref[1:2, lanes]
                                + x_next * cw_ref[2:3, lanes]))
            yield
        tiles.append(jnp.concatenate(halves, axis=0))
    q = jnp.concatenate(tiles[:C_W // LANES], axis=1)
    k = jnp.concatenate(tiles[C_W // LANES:], axis=1) * (C_DH ** -0.5)
    v = pc_ref[:, 2 * C_W:3 * C_W]
    head_of_lane = _div(_iota((blk, C_W), 1), C_DH)
    qk = []
    for h in range(C_HEADS):
        qk.append(_dot_nt(q, jnp.where(head_of_lane == h, k, 0.0)))
        yield
    return q, k, v, qk


def _mlstm_gates(ps_ref, sel_ref, fb_ref, rows_ref, stat_ref, *, rev):
    sub, blk, nh = SUBLANES, SEQ_BLOCK, C_HEADS
    g = _sel_dot_nt(sel_ref[...], ps_ref[...])
    li = g[0:sub]
    lf = jnp.where(_iota((sub, blk), 0) < nh, _log_sigmoid(g[sub:2 * sub] + fb_ref[:, 0:1]), 0.0)
    yield
    src_tok = _iota((blk, blk), 0)
    dst_tok = _iota((blk, blk), 1)
    upto = (src_tok >= dst_tok) if rev else (src_tok <= dst_tok)
    bcum = _dot_sel_rhs(lf, upto.astype(F32))
    yield
    btot = jnp.sum(lf, axis=1, keepdims=True)
    u = li - bcum
    u_max = yield from _lane_scan(u, jnp.maximum, -jnp.inf, rev)
    g_end = btot - bcum + li
    m_loc = jnp.max(g_end, axis=1, keepdims=True)
    rows_ref[...] = jnp.concatenate([u, u_max, bcum, jnp.exp(g_end - m_loc)], axis=0)
    stat_ref[...] = jnp.concatenate([jnp.broadcast_to(btot, (sub, LANES)),
                                     jnp.broadcast_to(m_loc, (sub, LANES))], axis=0)


def _mlstm_step(qkv, rows_ref, stat_ref, o_ref, c_ref, n_ref, m_ref, *, rev):
    blk = SEQ_BLOCK
    nh, dh = C_HEADS, C_DH
    sub = SUBLANES
    q, k, v, qk = qkv
    u = rows_ref[0:sub, :]
    u_max = rows_ref[sub:2 * sub, :]
    bcum = rows_ref[2 * sub:3 * sub, :]
    w_end = rows_ref[3 * sub:4 * sub, :]
    btot = stat_ref[0:sub, 0:1]
    m_loc = stat_ref[sub:2 * sub, 0:1]
    yield

    m_prev = m_ref[:, 0:1]
    big_m = jnp.maximum(u_max, m_prev)
    a_t = jnp.exp(m_prev - big_m)
    neg_m_t = -(bcum + big_m)
    m_new = jnp.maximum(btot + m_prev, m_loc)
    a_st = jnp.exp(btot + m_prev - m_new)
    e_st = jnp.exp(m_loc - m_new)

    pad = jnp.zeros((LANES - 4 * sub, blk), F32)
    yield
    cols = jnp.concatenate([big_m, a_t, neg_m_t, w_end, pad], axis=0).T
    yield
    width = 3 * C_W
    src_row = _iota((LANES, width), 0)
    lane = _iota((LANES, width), 1)
    src = sub * (1 + _div(lane, C_W)) + _div(lane & (C_W - 1), dh)
    wide = _dot_sel_rhs(cols, (src_row == src).astype(F32), parts=2)
    a_all = wide[:, 0:C_W]
    floor_all = jnp.exp(wide[:, C_W:2 * C_W])
    w_all = wide[:, 2 * C_W:]

    c_t = c_ref[...]
    n_row = n_ref[0:1, :]
    rr = _iota((blk, blk), 0)
    cc = _iota((blk, blk), 1)
    causal = (rr <= cc) if rev else (cc <= rr)
    head_of_lane = _div(_iota((blk, C_W), 1), dh)
    num = a_all * _dot(q, c_t)
    row_sum = None
    for h in range(nh):
        in_head = head_of_lane == h
        e = jnp.where(causal, u[h:h + 1, :] - cols[:, h:h + 1], -jnp.inf)
        sc = (qk[h] * jnp.exp(e)).astype(BF16)
        num = num + jnp.dot(sc, jnp.where(in_head, v, 0.0).astype(BF16), preferred_element_type=F32)
        rs = jnp.dot(sc, in_head.astype(BF16), preferred_element_type=F32)
        row_sum = rs if row_sum is None else row_sum + rs
        yield
    same_head = _div(_iota((C_W, C_W), 0), dh) == _div(_iota((C_W, C_W), 1), dh)
    den = a_all * _dot(q * n_row, same_head) + row_sum
    o_ref[...] = num / jnp.maximum(jnp.abs(den), floor_all)
    yield

    head8 = _div(_iota((sub, C_W), 1), dh) == _iota((sub, C_W), 0)
    a_lane = jnp.sum(jnp.where(head8, a_st, 0.0), axis=0, keepdims=True)
    e_lane = jnp.sum(jnp.where(head8, e_st, 0.0), axis=0, keepdims=True)
    c_loc = _dot(k.T, v * w_all)
    c_ref[...] = a_lane * c_t + jnp.where(same_head, e_lane * c_loc, 0.0)
    n_ref[0:1, :] = a_lane * n_row + e_lane * jnp.sum(k * w_all, axis=0, keepdims=True)
    m_ref[...] = jnp.broadcast_to(m_new, m_ref.shape)


def _mlstm(pc, ps, conv_w, gate_sel, f_bias_rows, batch, seq, ctx_len, layer):
    n_tok = pc.shape[0]
    blk = SEQ_BLOCK
    ctx_blocks, seq_blocks = ctx_len // blk, seq // blk
    rows8 = blk // SUBLANES
    last8 = n_tok // SUBLANES - 1

    def chain_specs(reverse):
        def blk_of(b, j):
            return _seq_block(b, j, reverse, batch, ctx_blocks, seq_blocks)
        return [
            pl.BlockSpec((blk, 3 * C_W), lambda b, j: (blk_of(b, j), 0)),
            pl.BlockSpec((SUBLANES, 2 * C_W), lambda b, j: (jnp.maximum(blk_of(b, j) * rows8 - 1, 0), 0)),
            pl.BlockSpec((SUBLANES, 2 * C_W), lambda b, j: (jnp.minimum((blk_of(b, j) + 1) * rows8, last8), 0)),
            pl.BlockSpec((blk, PS_W), lambda b, j: (blk_of(b, j), 0)),
            pl.BlockSpec((blk, PS_W), lambda b, j: (blk_of(b, jnp.minimum(j + 1, last_step)), 0)),
        ], pl.BlockSpec((blk, C_W), lambda b, j: (blk_of(b, j), 0))

    last_step = ctx_blocks + seq_blocks - 1
    in_f, out_f = chain_specs(False)
    in_b, out_b = chain_specs(True)
    out = jax.ShapeDtypeStruct((n_tok, C_W), F32)
    return pl.pallas_call(
        functools.partial(_mlstm_kernel, ctx_blocks=ctx_blocks, seq_blocks=seq_blocks),
        out_shape=[out, out],
        grid=(batch, ctx_blocks + seq_blocks),
        in_specs=in_f + in_b + [
            pl.BlockSpec((None, 3, 2 * C_W), lambda b, j: (layer, 0, 0)),
            pl.BlockSpec((2, 2 * SUBLANES, PS_W), lambda b, j: (0, 0, 0)),
            pl.BlockSpec((None, 2, SUBLANES, LANES), lambda b, j: (layer, 0, 0, 0)),
        ],
        out_specs=[out_f, out_b],
        scratch_shapes=[pltpu.VMEM((2, C_W, C_W), F32), pltpu.VMEM((2, SUBLANES, C_W), F32),
                        pltpu.VMEM((2, SUBLANES, LANES), F32), pltpu.VMEM((2, 4 * SUBLANES, blk), F32),
                        pltpu.VMEM((2, 2 * SUBLANES, LANES), F32)],
        compiler_params=_params("parallel", "arbitrary"),
        name="mlstm",
    )(pc, pc, pc, ps, ps, pc, pc, pc, ps, ps, conv_w, gate_sel, f_bias_rows)


def _head_layernorm(o, g, width):
    n = o.shape[1]
    avg = jnp.where(_div(_iota((n, n), 0), width) == _div(_iota((n, n), 1), width), 1.0 / width, 0.0)
    mu = _dot_sel_rhs(o, avg, parts=2)
    cen = o - mu
    var = _dot_sel_rhs(cen * cen, avg, parts=2)
    return cen * lax.rsqrt(var + EPS) * g


def _out_kernel(x_ref, mod_ref, g_ref, ya_ref, gb_ref, obf_ref, obb_ref, gc_ref, ocf_ref, ocb_ref,
                gng_ref, mng_ref, w_ref, o_ref):
    y_b = _silu(gb_ref[...]) * _head_layernorm(obf_ref[...] + obb_ref[...], gng_ref[...], B_DV)
    y_c = _sigmoid(gc_ref[...]) * _head_layernorm(ocf_ref[...] + ocb_ref[...], mng_ref[...], C_DH)
    y = (_dot(ya_ref[...], w_ref[0:A_OUT, :])
         + _dot(y_b, w_ref[A_OUT:A_OUT + B_VW, :])
         + _dot(y_c, w_ref[A_OUT + B_VW:, :]))
    o_ref[...] = x_ref[...] + mod_ref[5:6, :] * _rms(y, g_ref[3:4, :])


def _out(tok, modtab, norm_g, ya, pb, ob_f, ob_b, pc, oc_f, oc_b, gla_norm, mlstm_norm, w_out, geom, layer):
    tm = geom.tm
    n_tok, d = tok.shape
    gb_col = (2 * B_KW + B_VW) // B_VW
    gc_col = 3
    return pl.pallas_call(
        _out_kernel,
        out_shape=jax.ShapeDtypeStruct((n_tok, d), F32),
        grid=(geom.tiles,),
        in_specs=[
            pl.BlockSpec((tm, d), lambda i: (i, 0)),
            pl.BlockSpec((None, None, N_MOD, d), lambda i: (layer, geom.mod_row(i), 0, 0)),
            pl.BlockSpec((None, 6, d), lambda i: (layer, 0, 0)),
            pl.BlockSpec((tm, A_OUT), lambda i: (i, 0)),
            pl.BlockSpec((tm, B_VW), lambda i: (i, gb_col)),
            pl.BlockSpec((tm, B_VW), lambda i: (i, 0)),
            pl.BlockSpec((tm, B_VW), lambda i: (i, 0)),
            pl.BlockSpec((tm, C_W), lambda i: (i, gc_col)),
            pl.BlockSpec((tm, C_W), lambda i: (i, 0)),
            pl.BlockSpec((tm, C_W), lambda i: (i, 0)),
            pl.BlockSpec((None, 1, B_VW), lambda i: (layer, 0, 0)),
            pl.BlockSpec((None, 1, C_W), lambda i: (layer, 0, 0)),
            pl.BlockSpec((None, d, d), lambda i: (layer, 0, 0)),
        ],
        out_specs=pl.BlockSpec((tm, d), lambda i: (i, 0)),
        compiler_params=_params("parallel"),
        name="out",
    )(tok, modtab, norm_g, ya, pb, ob_f, ob_b, pc, oc_f, oc_b, gla_norm, mlstm_norm, w_out)


def _rope_tables(seq, tm):
    pos = jnp.arange(seq)
    half = ROPE_AXIS // 2
    inv_freq = ROPE_BASE ** (-jnp.arange(half, dtype=F32) / half)
    ang_r = (pos // GRID_W).astype(F32)[:, None] * inv_freq[None, :]
    ang_c = (pos % GRID_W).astype(F32)[:, None] * inv_freq[None, :]
    cos = jnp.concatenate([jnp.cos(ang_r)] * 2 + [jnp.cos(ang_c)] * 2, axis=1)
    sin = jnp.concatenate([-jnp.sin(ang_r), jnp.sin(ang_r), -jnp.sin(ang_c), jnp.sin(ang_c)], axis=1)
    reps = LANES // A_HEAD_DIM
    cos = jnp.concatenate([jnp.tile(cos, (1, reps)), jnp.ones((tm, LANES), F32)], axis=0)
    sin = jnp.concatenate([jnp.tile(sin, (1, reps)), jnp.zeros((tm, LANES), F32)], axis=0)
    return cos, sin


def _regroup_in(w):
    a_end = PA_W + PB_W
    alpha = slice(a_end, a_end + 2 * GLA_RANK)
    c_qkv = slice(alpha.stop, alpha.stop + 3 * C_W)
    c_gate = slice(c_qkv.stop, c_qkv.stop + 4 * C_HEADS)
    c_og = slice(c_gate.stop, c_gate.stop + C_W)
    pad = jnp.zeros(w.shape[:-1] + (PS_W - 2 * GLA_RANK - 4 * C_HEADS,), w.dtype)
    return jnp.concatenate([w[..., :a_end], w[..., c_qkv], w[..., c_og], w[..., alpha], w[..., c_gate], pad],
                           axis=-1)


def _pick_tile(n_ctx, seq, largest):
    tm = largest
    while tm >= LANES:
        if n_ctx % tm == 0 and seq % tm == 0:
            return tm
        tm //= 2
    raise ValueError("token counts must be multiples of 128")


def kernel(x, c, ctx, c_ctx, ada_w, ada_b, norm_g, ffn1_wg, ffn1_wu, ffn1_wd, ffn2_wg, ffn2_wu, ffn2_wd,
           w_in, b_in, w_out, attn_sink, gla_w_alpha, gla_b_alpha, gla_norm, mlstm_conv, mlstm_f_bias,
           mlstm_norm):
    batch, seq, d = x.shape
    ctx_len = ctx.shape[1]
    n_l = ada_w.shape[0]
    assert seq % SEQ_BLOCK == 0 and ctx_len % SEQ_BLOCK == 0
    geom = _Geom(batch, seq, ctx_len, _pick_tile(batch * ctx_len, seq, TOKEN_TILE))
    geom_ffn = _Geom(batch, seq, ctx_len, _pick_tile(batch * ctx_len, seq, FFN_TOKEN_TILE))

    rows = -(-(batch + 1) // SUBLANES) * SUBLANES
    cc = jnp.concatenate([c, c_ctx[None, :], jnp.zeros((rows - batch - 1, d), F32)], axis=0)
    modtab = _ada_table(cc, ada_w, ada_b)

    rope_cos, rope_sin = _rope_tables(seq, geom.tm)
    w_in_r = _regroup_in(w_in.astype(BF16))
    b_in_r = _regroup_in(b_in)[:, None, :]
    wg1, wu1, wd1 = ffn1_wg.astype(BF16), ffn1_wu.astype(BF16), ffn1_wd.astype(BF16)
    wg2, wu2, wd2 = ffn2_wg.astype(BF16), ffn2_wu.astype(BF16), ffn2_wd.astype(BF16)
    w_out_b = w_out.astype(BF16)

    wa_pad = jnp.zeros((n_l, 2, PS_W, B_KW), F32)
    for dirn in range(2):
        wa_pad = wa_pad.at[:, dirn, dirn * GLA_RANK:(dirn + 1) * GLA_RANK, :].set(gla_w_alpha[:, dirn])
    wa_pad = wa_pad.astype(BF16)
    ba = gla_b_alpha[:, :, None, :]

    nh = C_HEADS
    out_idx = jnp.arange(2 * SUBLANES)[:, None]
    src_idx = jnp.arange(PS_W)[None, :]
    head_idx = out_idx % SUBLANES
    gate_sel = jnp.stack([
        (head_idx < nh) & (src_idx == PS_GATE0 + 2 * nh * dirn + nh * (out_idx // SUBLANES) + head_idx)
        for dirn in range(2)]).astype(F32)
    f_bias_rows = jnp.zeros((n_l, 2, SUBLANES), F32).at[:, :, 0:nh].set(mlstm_f_bias)
    f_bias_rows = jnp.broadcast_to(f_bias_rows[:, :, :, None], (n_l, 2, SUBLANES, LANES))

    sink_rows = jnp.broadcast_to(attn_sink[:, :, None], (n_l, A_HEADS, LANES))
    gla_g = gla_norm[:, None, :]
    mlstm_g = mlstm_norm[:, None, :]

    tok = jnp.concatenate([ctx.reshape(batch * ctx_len, d), x.reshape(batch * seq, d)], axis=0)
    for layer in range(n_l):
        last = layer == n_l - 1
        tok = _ffn(tok, modtab, norm_g, wg1, wu1, wd1, geom_ffn, layer, 0, 0)
        pa, pb, pc, ps = _inproj(tok, modtab, norm_g, w_in_r, b_in_r, rope_cos, rope_sin, geom, layer)
        ya = _attn(pa, sink_rows[layer], batch, seq, ctx_len)
        ob_f, ob_b = _gla(pb, ps, wa_pad, ba, batch, seq, ctx_len, layer)
        oc_f, oc_b = _mlstm(pc, ps, mlstm_conv, gate_sel, f_bias_rows, batch, seq, ctx_len, layer)
        tok = _out(tok, modtab, norm_g, ya, pb, ob_f, ob_b, pc, oc_f, oc_b, gla_g, mlstm_g, w_out_b, geom, layer)
        tok = _ffn(tok, modtab, norm_g, wg2, wu2, wd2, geom_ffn, layer, 6, 4, latent_only=last)
    return tok.reshape(batch, seq, d)
```
